```python
import math
import jax, jax.numpy as jnp
from jax import lax
import numpy as np

D_MODEL = 1024
BATCH = 8
SEQ = 2048
DEPTH = 1

GRID_W = 64
CTX_LEN = 256
EPS = 1e-6

NA_HEADS = 8
HEAD_DIM = 64
NA_WIDTH = NA_HEADS * HEAD_DIM
NA_ROWS = 8
NA_COLS = 16
ROPE_THETA = 10000.0

SSD_HEADS = 8
SSD_HEAD_DIM = 64
SSD_WIDTH = SSD_HEADS * SSD_HEAD_DIM
SSD_GROUPS = 2
SSD_STATE = 128
SSD_CONV = 3
SSD_CHUNK = 128
GN = SSD_GROUPS * SSD_STATE
XBC_WIDTH = SSD_WIDTH + 2 * GN

MIX_WIDTH = NA_WIDTH + SSD_WIDTH
IN_WIDTH = 3 * NA_WIDTH + SSD_WIDTH + XBC_WIDTH + 2 * SSD_HEADS

D_FF = 2816
FFN_CONV = 3

kernel_name = "hymba_na_ssd_convglu_prefix_dit"


def rms_norm(x, w):
    xf = x.astype(jnp.float32)
    y = xf * lax.rsqrt(jnp.mean(xf * xf, axis=-1, keepdims=True) + EPS)
    return (y * w.astype(jnp.float32)).astype(x.dtype)


def modulate(h, shift, scale):
    return h * (1.0 + scale) + shift


def flip(t):
    return jnp.flip(t, axis=1)


def depthwise_conv(x, w, b):
    k, ch = w.shape
    y = lax.conv_general_dilated(
        x, w[:, None, :].astype(x.dtype), window_strides=(1,),
        padding=[(k // 2, k // 2)], dimension_numbers=("NWC", "WIO", "NWC"),
        feature_group_count=ch)
    return y + b


def axial_rope(x, row_pos, col_pos):
    n_freq = x.shape[-1] // 4
    freqs = ROPE_THETA ** (-jnp.arange(n_freq, dtype=jnp.float32) / n_freq)
    ang = jnp.concatenate([row_pos[:, None] * freqs, col_pos[:, None] * freqs], axis=-1)
    cos = jnp.cos(ang)[None, :, None, :]
    sin = jnp.sin(ang)[None, :, None, :]
    x1, x2 = jnp.split(x.astype(jnp.float32), 2, axis=-1)
    return jnp.concatenate([x1 * cos - x2 * sin, x1 * sin + x2 * cos], axis=-1).astype(x.dtype)


def neighbourhood_attention(q, k, v, k_ctx, v_ctx, rpb, rows):
    bsz, s, nh, dh = q.shape
    kh = min(NA_ROWS, rows)
    scale = dh ** -0.5
    qg = q.reshape(bsz, rows, GRID_W, nh, dh)
    kg = k.reshape(bsz, rows, GRID_W, nh, dh)
    vg = v.reshape(bsz, rows, GRID_W, nh, dh)
    col = np.arange(GRID_W)
    col_start = np.clip(col - NA_COLS // 2, 0, GRID_W - NA_COLS)
    col_idx = col_start[:, None] + np.arange(NA_COLS)
    dc = col_idx - col[:, None] + (NA_COLS - 1)
    n_loc = kh * NA_COLS

    def row_block(r):
        rs = jnp.clip(r - kh // 2, 0, rows - kh)
        k_win = lax.dynamic_slice_in_dim(kg, rs, kh, axis=1)[:, :, col_idx]
        v_win = lax.dynamic_slice_in_dim(vg, rs, kh, axis=1)[:, :, col_idx]
        q_r = lax.dynamic_index_in_dim(qg, r, axis=1, keepdims=False)
        dr = rs + jnp.arange(kh) - r + (NA_ROWS - 1)
        bias = rpb[:, dr[None, :, None], dc[:, None, :]]
        s_loc = jnp.einsum("bwhd,brwkhd->bhwrk", q_r, k_win) * scale + bias
        s_ctx = jnp.einsum("bwhd,bchd->bhwc", q_r, k_ctx) * scale
        scores = jnp.concatenate([s_loc.reshape(bsz, nh, GRID_W, n_loc), s_ctx], axis=-1)
        p = jax.nn.softmax(scores.astype(jnp.float32), axis=-1).astype(v.dtype)
        p_loc = p[..., :n_loc].reshape(bsz, nh, GRID_W, kh, NA_COLS)
        return (jnp.einsum("bhwrk,brwkhd->bwhd", p_loc, v_win)
                + jnp.einsum("bhwc,bchd->bwhd", p[..., n_loc:], v_ctx))

    out = lax.map(row_block, jnp.arange(rows))
    return out.transpose(1, 0, 2, 3, 4).reshape(bsz, s, nh * dh)


def context_attention(q, k, v):
    bsz, lc, nh, dh = q.shape
    s = jnp.einsum("bqhd,bkhd->bhqk", q, k) * dh ** -0.5
    p = jax.nn.softmax(s.astype(jnp.float32), axis=-1).astype(v.dtype)
    return jnp.einsum("bhqk,bkhd->bqhd", p, v).reshape(bsz, lc, nh * dh)


def ssd_prep(p_xbc_dt, conv_w, conv_b, dt_bias):
    bsz, l, _ = p_xbc_dt.shape
    xbc = jax.nn.silu(depthwise_conv(p_xbc_dt[..., :XBC_WIDTH], conv_w, conv_b))
    xs = xbc[..., :SSD_WIDTH].reshape(bsz, l, SSD_HEADS, SSD_HEAD_DIM)
    bm = xbc[..., SSD_WIDTH:SSD_WIDTH + GN].reshape(bsz, l, SSD_GROUPS, SSD_STATE)
    cm = xbc[..., SSD_WIDTH + GN:].reshape(bsz, l, SSD_GROUPS, SSD_STATE)
    dt_raw = p_xbc_dt[..., XBC_WIDTH:].reshape(bsz, l, 2, SSD_HEADS).astype(jnp.float32)
    dt = jax.nn.softplus(dt_raw + dt_bias.astype(jnp.float32))
    return xs, bm, cm, dt


def ssd_chunked(xs, dt, a, bm, cm, h0):
    bsz, l, nh, hp = xs.shape
    g, n = bm.shape[2], bm.shape[3]
    e = nh // g
    q = SSD_CHUNK
    nc = l // q
    xc = xs.reshape(bsz, nc, q, g, e, hp)
    bc = bm.reshape(bsz, nc, q, g, n)
    cc = cm.reshape(bsz, nc, q, g, n)
    dtc = dt.reshape(bsz, nc, q, g, e)
    cum = jnp.cumsum(dtc * a.reshape(g, e), axis=2)
    seg = cum[:, :, :, None] - cum[:, :, None, :]
    lower = np.tril(np.ones((q, q), dtype=bool))[:, :, None, None]
    decay = jnp.exp(jnp.where(lower, seg, -jnp.inf))
    cb = jnp.einsum("bcign,bcjgn->bcijg", cc, bc)
    m = cb[..., None] * decay * dtc[:, :, None]
    y_diag = jnp.einsum("bcijge,bcjgep->bcigep", m, xc)
    w_end = jnp.exp(cum[:, :, -1:] - cum) * dtc
    states = jnp.einsum("bcjgn,bcjge,bcjgep->bcgepn", bc, w_end, xc)
    chunk_decay = jnp.exp(cum[:, :, -1])

    def step(h, inp):
        dec, st = inp
        return h * dec[..., None, None] + st, h

    h_final, h_in = lax.scan(step, h0.reshape(bsz, g, e, hp, n).astype(states.dtype),
                             (jnp.moveaxis(chunk_decay, 1, 0), jnp.moveaxis(states, 1, 0)))
    h_in = jnp.moveaxis(h_in, 0, 1)
    y_off = jnp.einsum("bcign,bcgepn,bcige->bcigep", cc, h_in, jnp.exp(cum))
    return (y_diag + y_off).reshape(bsz, l, nh, hp), h_final.reshape(bsz, nh, hp, n)


def ssd_final_state(xs, dt, a, bm):
    bsz, l, nh, hp = xs.shape
    g, n = bm.shape[2], bm.shape[3]
    e = nh // g
    cum = jnp.cumsum(dt * a, axis=1)
    w = (jnp.exp(cum[:, -1:] - cum) * dt).reshape(bsz, l, g, e)
    h = jnp.einsum("blgn,blge,blgep->bgepn", bm, w, xs.reshape(bsz, l, g, e, hp))
    return h.reshape(bsz, nh, hp, n)


def ssd_bidirectional(z, xs, bm, cm, dt, a, d_skip, norm_w, h0_fwd, h0_bwd):
    y_f, h_f = ssd_chunked(xs, dt[..., 0, :], a[0], bm, cm, h0_fwd)
    y_b, h_b = ssd_chunked(flip(xs), flip(dt[..., 1, :]), a[1], flip(bm), flip(cm), h0_bwd)
    y = y_f + flip(y_b) + d_skip[:, None] * xs
    bsz, l = xs.shape[:2]
    y = y.reshape(bsz, l, SSD_WIDTH) * jax.nn.silu(z)
    return rms_norm(y, norm_w), h_f, h_b


def conv_glu(h, w_up, conv_w, conv_b, w_down):
    gate, val = jnp.split(h @ w_up, 2, axis=-1)
    gate = depthwise_conv(gate, conv_w, conv_b)
    return (jax.nn.silu(gate) * val) @ w_down


def hybrid_layer(x, ctx, c, c_ctx, ada_w, ada_b, norm1_w, w_in, rpb, ssd_conv_w, ssd_conv_b,
                 dt_bias, a_log, ssd_d, ssd_norm_w, w_out, norm2_w, ffn_w_up, ffn_conv_w,
                 ffn_conv_b, ffn_w_down, rows, row_pos, col_pos, update_ctx):
    bsz, s, _ = x.shape
    lc = ctx.shape[1]
    mod = (jax.nn.silu(c) @ ada_w + ada_b)[:, None, :]
    mod_c = (jax.nn.silu(c_ctx) @ ada_w + ada_b)[None, None, :]
    sh1, sc1, g1, sh2, sc2, g2 = jnp.split(mod, 6, axis=-1)
    sh1c, sc1c, g1c, sh2c, sc2c, g2c = jnp.split(mod_c, 6, axis=-1)
    a = -jnp.exp(a_log.astype(jnp.float32))

    h = modulate(rms_norm(x, norm1_w), sh1, sc1)
    hc = modulate(rms_norm(ctx, norm1_w), sh1c, sc1c)
    p = h @ w_in
    col_ssd = 3 * NA_WIDTH + SSD_WIDTH
    if update_ctx:
        pc = hc @ w_in
        qc = pc[..., :NA_WIDTH].reshape(bsz, lc, NA_HEADS, HEAD_DIM)
        kvc = pc[..., NA_WIDTH:3 * NA_WIDTH]
        zc = pc[..., 3 * NA_WIDTH:col_ssd]
        sc_in = pc[..., col_ssd:]
    else:
        kvc = hc @ w_in[:, NA_WIDTH:3 * NA_WIDTH]
        sc_in = hc @ w_in[:, col_ssd:]
    kc = kvc[..., :NA_WIDTH].reshape(bsz, lc, NA_HEADS, HEAD_DIM)
    vc = kvc[..., NA_WIDTH:].reshape(bsz, lc, NA_HEADS, HEAD_DIM)

    q = axial_rope(p[..., :NA_WIDTH].reshape(bsz, s, NA_HEADS, HEAD_DIM), row_pos, col_pos)
    k = axial_rope(p[..., NA_WIDTH:2 * NA_WIDTH].reshape(bsz, s, NA_HEADS, HEAD_DIM), row_pos, col_pos)
    v = p[..., 2 * NA_WIDTH:3 * NA_WIDTH].reshape(bsz, s, NA_HEADS, HEAD_DIM)
    attn = neighbourhood_attention(q, k, v, kc, vc, rpb, rows)

    xs_c, bm_c, cm_c, dt_c = ssd_prep(sc_in, ssd_conv_w, ssd_conv_b, dt_bias)
    if update_ctx:
        zero = jnp.zeros((bsz, SSD_HEADS, SSD_HEAD_DIM, SSD_STATE), jnp.float32)
        y_c, h_fwd, h_bwd = ssd_bidirectional(zc, xs_c, bm_c, cm_c, dt_c, a, ssd_d, ssd_norm_w, zero, zero)
    else:
        h_fwd = ssd_final_state(xs_c, dt_c[..., 0, :], a[0], bm_c)
        h_bwd = ssd_final_state(flip(xs_c), flip(dt_c[..., 1, :]), a[1], flip(bm_c))
    xs, bm, cm, dt = ssd_prep(p[..., col_ssd:], ssd_conv_w, ssd_conv_b, dt_bias)
    y_ssd, _, _ = ssd_bidirectional(p[..., 3 * NA_WIDTH:col_ssd], xs, bm, cm, dt, a, ssd_d,
                                    ssd_norm_w, h_fwd, h_bwd)

    x = x + g1 * (jnp.concatenate([attn, y_ssd], axis=-1) @ w_out)
    x = x + g2 * conv_glu(modulate(rms_norm(x, norm2_w), sh2, sc2), ffn_w_up, ffn_conv_w, ffn_conv_b, ffn_w_down)

    if update_ctx:
        attn_c = context_attention(qc, kc, vc)
        ctx = ctx + g1c * (jnp.concatenate([attn_c, y_c], axis=-1) @ w_out)
        ctx = ctx + g2c * conv_glu(modulate(rms_norm(ctx, norm2_w), sh2c, sc2c), ffn_w_up, ffn_conv_w, ffn_conv_b, ffn_w_down)
    return x, ctx


def setup_inputs(seed: int = 0) -> dict:
    key = jax.random.key(seed)
    ks = jax.random.split(key, 24)
    f32 = jnp.float32

    def nrm(k, shape, s):
        return s * jax.random.normal(k, shape, f32)

    dt0 = jnp.exp(jax.random.uniform(ks[11], (DEPTH, 2, SSD_HEADS), f32, math.log(1e-3), math.log(1e-1)))
    return {
        "x": nrm(ks[0], (BATCH, SEQ, D_MODEL), 1.0),
        "c": nrm(ks[1], (BATCH, D_MODEL), 1.0),
        "ctx": nrm(ks[2], (BATCH, CTX_LEN, D_MODEL), 1.0),
        "c_ctx": nrm(ks[3], (D_MODEL,), 1.0),
        "ada_w": nrm(ks[4], (DEPTH, D_MODEL, 6 * D_MODEL), 0.5 * D_MODEL ** -0.5),
        "ada_b": nrm(ks[5], (DEPTH, 6 * D_MODEL), 0.02),
        "norm1_w": 1.0 + nrm(ks[6], (DEPTH, D_MODEL), 0.1),
        "w_in": nrm(ks[7], (DEPTH, D_MODEL, IN_WIDTH), D_MODEL ** -0.5),
        "rpb": nrm(ks[8], (DEPTH, NA_HEADS, 2 * NA_ROWS - 1, 2 * NA_COLS - 1), 0.5),
        "ssd_conv_w": nrm(ks[9], (DEPTH, SSD_CONV, XBC_WIDTH), SSD_CONV ** -0.5),
        "ssd_conv_b": nrm(ks[10], (DEPTH, XBC_WIDTH), 0.01),
        "dt_bias": dt0 + jnp.log(-jnp.expm1(-dt0)),
        "a_log": jnp.log(jax.random.uniform(ks[12], (DEPTH, 2, SSD_HEADS), f32, 1.0, 16.0)),
        "ssd_d": 1.0 + nrm(ks[13], (DEPTH, SSD_HEADS), 0.1),
        "ssd_norm_w": 1.0 + nrm(ks[14], (DEPTH, SSD_WIDTH), 0.1),
        "w_out": nrm(ks[15], (DEPTH, MIX_WIDTH, D_MODEL), MIX_WIDTH ** -0.5),
        "norm2_w": 1.0 + nrm(ks[16], (DEPTH, D_MODEL), 0.1),
        "ffn_w_up": nrm(ks[17], (DEPTH, D_MODEL, 2 * D_FF), D_MODEL ** -0.5),
        "ffn_conv_w": nrm(ks[18], (DEPTH, FFN_CONV, D_FF), FFN_CONV ** -0.5),
        "ffn_conv_b": nrm(ks[19], (DEPTH, D_FF), 0.01),
        "ffn_w_down": nrm(ks[20], (DEPTH, D_FF, D_MODEL), D_FF ** -0.5),
        "final_norm_w": 1.0 + nrm(ks[21], (D_MODEL,), 0.1),
    }


def reference(x, c, ctx, c_ctx, ada_w, ada_b, norm1_w, w_in, rpb, ssd_conv_w, ssd_conv_b,
              dt_bias, a_log, ssd_d, ssd_norm_w, w_out, norm2_w, ffn_w_up, ffn_conv_w,
              ffn_conv_b, ffn_w_down, final_norm_w):
    s = x.shape[1]
    rows = s // GRID_W
    t = jnp.arange(s)
    row_pos = (t // GRID_W).astype(jnp.float32)
    col_pos = (t % GRID_W).astype(jnp.float32)
    for i in range(DEPTH):
        x, ctx = hybrid_layer(x, ctx, c, c_ctx, ada_w[i], ada_b[i], norm1_w[i], w_in[i], rpb[i],
                              ssd_conv_w[i], ssd_conv_b[i], dt_bias[i], a_log[i], ssd_d[i],
                              ssd_norm_w[i], w_out[i], norm2_w[i], ffn_w_up[i], ffn_conv_w[i],
                              ffn_conv_b[i], ffn_w_down[i], rows, row_pos, col_pos,
                              update_ctx=(i < DEPTH - 1))
    return rms_norm(x, final_norm_w)
```

```python
import functools
import math

import numpy as np
import jax
import jax.numpy as jnp
from jax import lax
from jax.experimental import pallas as pl
from jax.experimental.pallas import tpu as pltpu

F32 = jnp.float32
BF16 = jnp.bfloat16

D_MODEL = 1024
GRID_W = 64
EPS = 1e-6
NA_HEADS = 8
HEAD_DIM = 64
NA_WIDTH = NA_HEADS * HEAD_DIM
NA_ROWS = 8
NA_COLS = 16
ROPE_THETA = 10000.0
SSD_HEADS = 8
SSD_HEAD_DIM = 64
SSD_WIDTH = SSD_HEADS * SSD_HEAD_DIM
SSD_GROUPS = 2
SSD_STATE = 128
GN = SSD_GROUPS * SSD_STATE
XBC_WIDTH = SSD_WIDTH + 2 * GN
D_FF = 2816

LANES = 128
CHUNK = 128
HALO = 16
Q_ROWS = 4
BAND_ROWS = Q_ROWS + NA_ROWS - 1
NEG = -1e30
LOG2E = math.log2(math.e)
VMEM_LIMIT = 56 * 1024 * 1024

NT_DIMS = (((1,), (1,)), ((), ()))


def _silu(v):
    return v / (1.0 + jnp.exp(-v))


def _softplus(v):
    return jnp.maximum(v, 0.0) + jnp.log(1.0 + jnp.exp(-jnp.abs(v)))


def _rms(v, w):
    return v * lax.rsqrt(jnp.mean(v * v, axis=-1, keepdims=True) + EPS) * w


def _params(sem):
    return pltpu.CompilerParams(dimension_semantics=sem, vmem_limit_bytes=VMEM_LIMIT)


def _ada_kernel(c_ref, w_ref, b_ref, o_ref):
    a = _silu(c_ref[...])
    o_ref[...] = jnp.dot(a, w_ref[...], preferred_element_type=F32,
                         precision=lax.Precision.HIGHEST) + b_ref[...]


def _ada(cc, ada_w, ada_b):
    rows, d = cc.shape
    n = ada_w.shape[1]
    tn = 1024
    return pl.pallas_call(
        _ada_kernel,
        grid=(n // tn,),
        in_specs=[pl.BlockSpec((rows, d), lambda j: (0, 0)),
                  pl.BlockSpec((d, tn), lambda j: (0, j)),
                  pl.BlockSpec((1, tn), lambda j: (0, j))],
        out_specs=pl.BlockSpec((rows, tn), lambda j: (0, j)),
        out_shape=jax.ShapeDtypeStruct((rows, n), F32),
        compiler_params=_params(("arbitrary",)),
        name="ada",
    )(cc, ada_w, ada_b.reshape(1, n))


def _rope(v, cos, sin_lo, sin_hi):
    outs = []
    for g in range(v.shape[1] // LANES):
        vg = v[:, g * LANES:(g + 1) * LANES]
        outs.append(vg * cos + pltpu.roll(vg, LANES - 32, 1) * sin_lo + pltpu.roll(vg, 32, 1) * sin_hi)
    return jnp.concatenate(outs, axis=1)


def _inproj_kernel(*refs, latent, tm, seq):
    if latent:
        (x_ref, mod_ref, nw_ref, w_ref, wdt_ref, cos_ref, slo_ref, shi_ref,
         q_ref, k_ref, v_ref, z_ref, xbcp_ref, dt_ref) = refs
    else:
        (x_ref, mod_ref, nw_ref, w_ref, wdt_ref, k_ref, v_ref, xbcp_ref, dt_ref) = refs
    m = pl.program_id(1)
    mod = mod_ref[0]
    h = (_rms(x_ref[0], nw_ref[...]) * (1.0 + mod[1:2]) + mod[0:1]).astype(BF16)

    def proj(lo, hi):
        return jnp.dot(h, w_ref[:, lo:hi], preferred_element_type=F32)

    if latent:
        cos, slo, shi = cos_ref[...], slo_ref[...], shi_ref[...]
        q = _rope(proj(0, NA_WIDTH), cos, slo, shi) * (HEAD_DIM ** -0.5 * LOG2E)
        q_ref[0] = q.astype(BF16)
        k_ref[0] = _rope(proj(NA_WIDTH, 2 * NA_WIDTH), cos, slo, shi).astype(BF16)
        z_ref[0] = proj(3 * NA_WIDTH, 3 * NA_WIDTH + SSD_WIDTH).astype(BF16)
    else:
        k_ref[0] = proj(NA_WIDTH, 2 * NA_WIDTH).astype(BF16)
    v_ref[0] = proj(2 * NA_WIDTH, 3 * NA_WIDTH).astype(BF16)

    @pl.when(m == 0)
    def _():
        zeros = jnp.zeros((HALO, XBC_WIDTH), BF16)
        xbcp_ref[0, 0:HALO, :] = zeros
        xbcp_ref[0, seq + HALO:seq + 2 * HALO, :] = zeros

    col = 3 * NA_WIDTH + SSD_WIDTH
    xbcp_ref[0, pl.ds(pl.multiple_of(HALO + m * tm, HALO), tm), :] = proj(col, col + XBC_WIDTH).astype(BF16)
    dt_ref[0] = jnp.dot(h, wdt_ref[...], preferred_element_type=F32)


def _inproj(x, mod, norm_w, w_main, w_dt, tables, *, latent, tm):
    bsz, seq, d = x.shape
    nm = seq // tm
    per_batch_mod = mod.shape[0] == bsz
    kern = functools.partial(_inproj_kernel, latent=latent, tm=tm, seq=seq)
    in_specs = [
        pl.BlockSpec((1, tm, d), lambda b, m: (b, m, 0)),
        pl.BlockSpec((1, 6, d), (lambda b, m: (b, 0, 0)) if per_batch_mod else (lambda b, m: (0, 0, 0))),
        pl.BlockSpec((1, d), lambda b, m: (0, 0)),
        pl.BlockSpec(w_main.shape, lambda b, m: (0, 0)),
        pl.BlockSpec(w_dt.shape, lambda b, m: (0, 0)),
    ]
    args = [x, mod, norm_w.reshape(1, d), w_main, w_dt]
    tile = lambda width, dt: (pl.BlockSpec((1, tm, width), lambda b, m: (b, m, 0)),
                              jax.ShapeDtypeStruct((bsz, seq, width), dt))
    xbcp = (pl.BlockSpec((1, seq + 2 * HALO, XBC_WIDTH), lambda b, m: (b, 0, 0)),
            jax.ShapeDtypeStruct((bsz, seq + 2 * HALO, XBC_WIDTH), BF16))
    if latent:
        in_specs += [pl.BlockSpec((tm, LANES), lambda b, m: (m, 0))] * 3
        args += list(tables)
        outs = [tile(NA_WIDTH, BF16), tile(NA_WIDTH, BF16), tile(NA_WIDTH, BF16), tile(SSD_WIDTH, BF16),
                xbcp, tile(LANES, F32)]
    else:
        outs = [tile(NA_WIDTH, BF16), tile(NA_WIDTH, BF16), xbcp, tile(LANES, F32)]
    return pl.pallas_call(
        kern,
        grid=(bsz, nm),
        in_specs=in_specs,
        out_specs=[o[0] for o in outs],
        out_shape=[o[1] for o in outs],
        compiler_params=_params(("arbitrary", "arbitrary")),
        name="inproj_latent" if latent else "inproj_ctx",
    )(*args)


def _rope_tables(seq):
    t = np.arange(seq)
    row = (t // GRID_W).astype(np.float32)
    col = (t % GRID_W).astype(np.float32)
    n_freq = HEAD_DIM // 4
    freqs = jnp.asarray(ROPE_THETA, F32) ** (-jnp.arange(n_freq, dtype=F32) / n_freq)
    ang = jnp.concatenate([jnp.asarray(row)[:, None] * freqs, jnp.asarray(col)[:, None] * freqs], axis=-1)
    cos, sin = jnp.cos(ang), jnp.sin(ang)
    zero = jnp.zeros_like(sin)
    rep = LANES // HEAD_DIM
    cos_t = jnp.tile(jnp.concatenate([cos, cos], axis=-1), (1, rep))
    sin_lo = jnp.tile(jnp.concatenate([-sin, zero], axis=-1), (1, rep))
    sin_hi = jnp.tile(jnp.concatenate([zero, sin], axis=-1), (1, rep))
    return cos_t, sin_lo, sin_hi


def _band_start(kb, rows):
    return min(max(Q_ROWS * kb - NA_ROWS // 2, 0), rows - BAND_ROWS)


def _bias_tables(rpb, rows):
    nblk = rows // Q_ROWS
    kh = min(NA_ROWS, rows)
    reps = (0, 1, nblk - 1)
    qr = np.arange(Q_ROWS)[:, None, None, None]
    w = np.arange(GRID_W)[None, :, None, None]
    bi = np.arange(BAND_ROWS)[None, None, :, None]
    c = np.arange(GRID_W)[None, None, None, :]
    cs = np.clip(w - NA_COLS // 2, 0, GRID_W - NA_COLS)
    col_ok = (c >= cs) & (c < cs + NA_COLS)
    dc = np.clip(c - w + (NA_COLS - 1), 0, 2 * NA_COLS - 2)
    idx_dr, idx_dc, valid = [], [], []
    for kb in reps:
        r = Q_ROWS * kb + qr
        rs = np.clip(r - kh // 2, 0, rows - kh)
        krow = _band_start(kb, rows) + bi
        row_ok = (krow >= rs) & (krow < rs + kh)
        dr = np.clip(krow - r + (NA_ROWS - 1), 0, 2 * NA_ROWS - 2)
        shape = (Q_ROWS, GRID_W, BAND_ROWS, GRID_W)
        idx_dr.append(np.broadcast_to(dr, shape).reshape(Q_ROWS * GRID_W, BAND_ROWS * GRID_W))
        idx_dc.append(np.broadcast_to(dc, shape).reshape(Q_ROWS * GRID_W, BAND_ROWS * GRID_W))
        valid.append(np.broadcast_to(row_ok & col_ok, shape).reshape(Q_ROWS * GRID_W, BAND_ROWS * GRID_W))
    idx_dr, idx_dc, valid = np.stack(idx_dr), np.stack(idx_dc), np.stack(valid)
    gathered = rpb[:, idx_dr, idx_dc] * LOG2E
    return jnp.where(valid[None], gathered, NEG)


def _attn_kernel(q_ref, k_ref, v_ref, kc_ref, vc_ref, bias_ref, o_ref, *, rows):
    kb = pl.program_id(2)
    nblk = rows // Q_ROWS
    start = jnp.clip(Q_ROWS * kb - NA_ROWS // 2, 0, rows - BAND_ROWS) * GRID_W
    start = pl.multiple_of(start, GRID_W)
    variant = jnp.where(kb == 0, 0, jnp.where(kb == nblk - 1, 2, 1))
    band = BAND_ROWS * GRID_W
    q = q_ref[0]
    kband = k_ref[0, pl.ds(start, band), :]
    vband = v_ref[0, pl.ds(start, band), :]
    kc = kc_ref[0]
    vc = vc_ref[0]
    lane = lax.broadcasted_iota(jnp.int32, (1, LANES), 1)
    out = None
    for hh in range(LANES // HEAD_DIM):
        sel = (lane >= hh * HEAD_DIM) & (lane < (hh + 1) * HEAD_DIM)
        qm = jnp.where(sel, q, jnp.zeros_like(q))
        s_loc = lax.dot_general(qm, kband, NT_DIMS, preferred_element_type=F32) + bias_ref[hh, variant]
        s_ctx = lax.dot_general(qm, kc, NT_DIMS, preferred_element_type=F32)
        mx = jnp.maximum(jnp.max(s_loc, axis=-1, keepdims=True), jnp.max(s_ctx, axis=-1, keepdims=True))
        p_loc = jnp.exp2(s_loc - mx)
        p_ctx = jnp.exp2(s_ctx - mx)
        den = jnp.sum(p_loc, axis=-1, keepdims=True) + jnp.sum(p_ctx, axis=-1, keepdims=True)
        o = (jnp.dot(p_loc.astype(BF16), vband, preferred_element_type=F32)
             + jnp.dot(p_ctx.astype(BF16), vc, preferred_element_type=F32)) / den
        out = o if out is None else jnp.where(sel, o, out)
    o_ref[0] = out.astype(BF16)


def _attention(q, k, v, kc, vc, bias):
    bsz, seq, width = q.shape
    lc = kc.shape[1]
    rows = seq // GRID_W
    nblk = rows // Q_ROWS
    tq = Q_ROWS * GRID_W
    npair = width // LANES
    hpp = LANES // HEAD_DIM
    return pl.pallas_call(
        functools.partial(_attn_kernel, rows=rows),
        grid=(npair, bsz, nblk),
        in_specs=[
            pl.BlockSpec((1, tq, LANES), lambda p, b, kb: (b, kb, p)),
            pl.BlockSpec((1, seq, LANES), lambda p, b, kb: (b, 0, p)),
            pl.BlockSpec((1, seq, LANES), lambda p, b, kb: (b, 0, p)),
            pl.BlockSpec((1, lc, LANES), lambda p, b, kb: (b, 0, p)),
            pl.BlockSpec((1, lc, LANES), lambda p, b, kb: (b, 0, p)),
            pl.BlockSpec((hpp, 3, tq, BAND_ROWS * GRID_W), lambda p, b, kb: (p, 0, 0, 0)),
        ],
        out_specs=pl.BlockSpec((1, tq, LANES), lambda p, b, kb: (b, kb, p)),
        out_shape=jax.ShapeDtypeStruct((bsz, seq, width), BF16),
        compiler_params=_params(("arbitrary", "arbitrary", "arbitrary")),
        name="attn",
    )(q, k, v, kc, vc, bias)


def _ssd_kernel(xbcp_ref, dt_ref, dtt_ref, z_ref, xbcpc_ref, dtc_ref, dttc_ref,
                cw_ref, cb_ref, brow_ref, bcol_ref, arow_ref, acol_ref, dskip_ref, nw_ref,
                y_ref, xact, xactc, yacc, state, *, seq, lc):
    nchunk = seq // CHUNK
    nchunk_c = lc // CHUNK
    win = CHUNK + 2 * HALO

    def conv_pass(src_ref, dst_ref, n):
        def body(c, carry):
            r0 = pl.multiple_of(c * CHUNK, CHUNK)
            w = src_ref[0, pl.ds(r0, win), :].astype(F32)
            prev = pltpu.roll(w, 1, 0)[HALO:HALO + CHUNK]
            nxt = pltpu.roll(w, win - 1, 0)[HALO:HALO + CHUNK]
            g = cw_ref[0:1] * prev + cw_ref[1:2] * w[HALO:HALO + CHUNK] + cw_ref[2:3] * nxt + cb_ref[...]
            dst_ref[pl.ds(r0, CHUNK), :] = _silu(g).astype(BF16)
            return carry
        lax.fori_loop(0, n, body, 0)

    conv_pass(xbcp_ref, xact, nchunk)
    conv_pass(xbcpc_ref, xactc, nchunk_c)
    state[...] = jnp.zeros_like(state)

    a_row = -jnp.exp(arow_ref[...])
    a_col = -jnp.exp(acol_ref[...])
    ii = lax.broadcasted_iota(jnp.int32, (CHUNK, CHUNK), 0)
    jj = lax.broadcasted_iota(jnp.int32, (CHUNK, CHUNK), 1)
    lane_head = lax.broadcasted_iota(jnp.int32, (1, SSD_WIDTH // SSD_GROUPS), 1) // SSD_HEAD_DIM
    hpg = SSD_HEADS // SSD_GROUPS

    def chunk_step(xa, dtr, dtrt, d, need_y):
        dt = _softplus(dtr + brow_ref[...])
        dtt = _softplus(dtrt + bcol_ref[...])
        tri = (jj <= ii) if d == 0 else (jj >= ii)
        tri_f = tri.astype(F32)
        tri_t = ((ii <= jj) if d == 0 else (ii >= jj)).astype(F32)
        cum = jnp.dot(tri_f, dt * a_row, preferred_element_type=F32, precision=lax.Precision.HIGHEST)
        cumt = jnp.dot(dtt * a_col, tri_t, preferred_element_type=F32, precision=lax.Precision.HIGHEST)
        end = CHUNK - 1 if d == 0 else 0
        tot_t = cumt[:, end:end + 1]
        w_t = jnp.exp(tot_t - cumt) * dtt
        cd_t = jnp.exp(tot_t)
        e_in = jnp.exp(cum)
        ys = []
        for g in range(SSD_GROUPS):
            xg = xa[:, g * 256:(g + 1) * 256]
            bg = xa[:, SSD_WIDTH + g * SSD_STATE:SSD_WIDTH + (g + 1) * SSD_STATE]
            cg = xa[:, SSD_WIDTH + GN + g * SSD_STATE:SSD_WIDTH + GN + (g + 1) * SSD_STATE]
            xt = xg.astype(F32).T
            sg = state[d, g]
            if need_y:
                cb = lax.dot_general(cg, bg, NT_DIMS, preferred_element_type=F32)
                rhs_t = jnp.concatenate([xt.astype(BF16), sg.astype(BF16)], axis=1)
                cf = cg.astype(F32)
                yg = None
                for e in range(hpg):
                    hcol = SSD_HEADS * d + hpg * g + e
                    seg = cum[:, hcol:hcol + 1] - cumt[hcol:hcol + 1, :]
                    dec = jnp.exp(jnp.where(tri, seg, NEG))
                    mh = cb * dec * dtt[hcol:hcol + 1, :]
                    ch = cf * e_in[:, hcol:hcol + 1]
                    lhs = jnp.concatenate([mh, ch], axis=1).astype(BF16)
                    yd = lax.dot_general(lhs, rhs_t, NT_DIMS, preferred_element_type=F32)
                    yg = yd if yg is None else jnp.where(lane_head == e, yd, yg)
                ys.append(yg)
            h0 = SSD_HEADS * d + hpg * g
            wexp = jnp.concatenate(
                [jnp.broadcast_to(w_t[h0 + e:h0 + e + 1, :], (SSD_HEAD_DIM, CHUNK)) for e in range(hpg)], axis=0)
            cdexp = jnp.concatenate(
                [jnp.broadcast_to(cd_t[h0 + e:h0 + e + 1, :], (SSD_HEAD_DIM, SSD_STATE)) for e in range(hpg)], axis=0)
            st = jnp.dot((xt * wexp).astype(BF16), bg, preferred_element_type=F32)
            state[d, g] = sg * cdexp + st
        return jnp.concatenate(ys, axis=1) if need_y else None

    def ctx_body(d):
        def body(i, carry):
            c = i if d == 0 else nchunk_c - 1 - i
            r0 = pl.multiple_of(c * CHUNK, CHUNK)
            chunk_step(xactc[pl.ds(r0, CHUNK), :], dtc_ref[0, pl.ds(r0, CHUNK), :],
                       dttc_ref[0, :, pl.ds(r0, CHUNK)], d, False)
            return carry
        return body

    def fwd_body(c, carry):
        r0 = pl.multiple_of(c * CHUNK, CHUNK)
        yacc[pl.ds(r0, CHUNK), :] = chunk_step(
            xact[pl.ds(r0, CHUNK), :], dt_ref[0, pl.ds(r0, CHUNK), :], dtt_ref[0, :, pl.ds(r0, CHUNK)], 0, True)
        return carry

    def bwd_body(i, carry):
        c = nchunk - 1 - i
        r0 = pl.multiple_of(c * CHUNK, CHUNK)
        xa = xact[pl.ds(r0, CHUNK), :]
        yb = chunk_step(xa, dt_ref[0, pl.ds(r0, CHUNK), :], dtt_ref[0, :, pl.ds(r0, CHUNK)], 1, True)
        y = yacc[pl.ds(r0, CHUNK), :] + yb + dskip_ref[...] * xa[:, :SSD_WIDTH].astype(F32)
        y = y * _silu(z_ref[0, pl.ds(r0, CHUNK), :].astype(F32))
        y_ref[0, pl.ds(r0, CHUNK), :] = _rms(y, nw_ref[...]).astype(BF16)
        return carry

    lax.fori_loop(0, nchunk_c, ctx_body(0), 0)
    lax.fori_loop(0, nchunk, fwd_body, 0)
    lax.fori_loop(0, nchunk_c, ctx_body(1), 0)
    lax.fori_loop(0, nchunk, bwd_body, 0)


def _ssd(xbcp, dt, dtt, z, xbcp_c, dt_c, dtt_c, conv_w, conv_b, dt_bias, a_log, d_skip, norm_w):
    bsz, seq, _ = z.shape
    lc = dt_c.shape[1]
    nh2 = 2 * SSD_HEADS
    pad_row = lambda v: jnp.pad(v.reshape(1, nh2), ((0, 0), (0, LANES - nh2)))
    full = lambda shape: pl.BlockSpec(shape, lambda b: (0,) * len(shape))
    per_b = lambda shape: pl.BlockSpec((1,) + shape, lambda b: (b, 0, 0))
    return pl.pallas_call(
        functools.partial(_ssd_kernel, seq=seq, lc=lc),
        grid=(bsz,),
        in_specs=[
            per_b((seq + 2 * HALO, XBC_WIDTH)), per_b((seq, LANES)), per_b((nh2, seq)), per_b((seq, SSD_WIDTH)),
            per_b((lc + 2 * HALO, XBC_WIDTH)), per_b((lc, LANES)), per_b((nh2, lc)),
            full((3, XBC_WIDTH)), full((1, XBC_WIDTH)),
            full((1, LANES)), full((nh2, 1)), full((1, LANES)), full((nh2, 1)),
            full((1, SSD_WIDTH)), full((1, SSD_WIDTH)),
        ],
        out_specs=per_b((seq, SSD_WIDTH)),
        out_shape=jax.ShapeDtypeStruct((bsz, seq, SSD_WIDTH), BF16),
        scratch_shapes=[
            pltpu.VMEM((seq, XBC_WIDTH), BF16),
            pltpu.VMEM((lc, XBC_WIDTH), BF16),
            pltpu.VMEM((seq, SSD_WIDTH), F32),
            pltpu.VMEM((2, SSD_GROUPS, SSD_WIDTH // SSD_GROUPS, SSD_STATE), F32),
        ],
        compiler_params=_params(("arbitrary",)),
        name="ssd",
    )(xbcp, dt, dtt, z, xbcp_c, dt_c, dtt_c, conv_w, conv_b.reshape(1, XBC_WIDTH),
      pad_row(dt_bias), dt_bias.reshape(nh2, 1), pad_row(a_log), a_log.reshape(nh2, 1),
      jnp.repeat(d_skip, SSD_HEAD_DIM).reshape(1, SSD_WIDTH), norm_w.reshape(1, SSD_WIDTH))


def _outproj_kernel(attn_ref, yssd_ref, x_ref, mod_ref, wa_ref, ws_ref, nw_ref, x1_ref, h2p_ref, *, tm, seq):
    m = pl.program_id(1)
    mod = mod_ref[0]
    mix = (jnp.dot(attn_ref[0], wa_ref[...], preferred_element_type=F32)
           + jnp.dot(yssd_ref[0], ws_ref[...], preferred_element_type=F32))
    x1 = x_ref[0] + mod[2:3] * mix
    x1_ref[0] = x1
    h2 = _rms(x1, nw_ref[...]) * (1.0 + mod[4:5]) + mod[3:4]

    @pl.when(m == 0)
    def _():
        zeros = jnp.zeros((HALO, D_MODEL), BF16)
        h2p_ref[0, 0:HALO, :] = zeros
        h2p_ref[0, seq + HALO:seq + 2 * HALO, :] = zeros

    h2p_ref[0, pl.ds(pl.multiple_of(HALO + m * tm, HALO), tm), :] = h2.astype(BF16)


def _outproj(attn, yssd, x, mod, w_attn, w_ssd, norm_w, *, tm):
    bsz, seq, d = x.shape
    return pl.pallas_call(
        functools.partial(_outproj_kernel, tm=tm, seq=seq),
        grid=(bsz, seq // tm),
        in_specs=[
            pl.BlockSpec((1, tm, NA_WIDTH), lambda b, m: (b, m, 0)),
            pl.BlockSpec((1, tm, SSD_WIDTH), lambda b, m: (b, m, 0)),
            pl.BlockSpec((1, tm, d), lambda b, m: (b, m, 0)),
            pl.BlockSpec((1, 6, d), lambda b, m: (b, 0, 0)),
            pl.BlockSpec(w_attn.shape, lambda b, m: (0, 0)),
            pl.BlockSpec(w_ssd.shape, lambda b, m: (0, 0)),
            pl.BlockSpec((1, d), lambda b, m: (0, 0)),
        ],
        out_specs=[pl.BlockSpec((1, tm, d), lambda b, m: (b, m, 0)),
                   pl.BlockSpec((1, seq + 2 * HALO, d), lambda b, m: (b, 0, 0))],
        out_shape=[jax.ShapeDtypeStruct((bsz, seq, d), F32),
                   jax.ShapeDtypeStruct((bsz, seq + 2 * HALO, d), BF16)],
        compiler_params=_params(("arbitrary", "arbitrary")),
        name="outproj",
    )(attn, yssd, x, mod, w_attn, w_ssd, norm_w.reshape(1, d))


def _ffn_kernel(h2p_ref, x1_ref, mod_ref, wg_ref, wv_ref, wd_ref, cw_ref, cb_ref, fnw_ref, o_ref, *, tm, ts):
    m = pl.program_id(1)
    f = pl.program_id(2)
    nf = pl.num_programs(2)
    win = ts + 2 * HALO
    for r in range(tm // ts):
        start = pl.multiple_of(m * tm + r * ts, ts)
        hb = h2p_ref[0, pl.ds(start, win), :]
        gate = jnp.dot(hb, wg_ref[...], preferred_element_type=F32)
        val = jnp.dot(hb[HALO:HALO + ts], wv_ref[...], preferred_element_type=F32)
        prev = pltpu.roll(gate, 1, 0)[HALO:HALO + ts]
        nxt = pltpu.roll(gate, win - 1, 0)[HALO:HALO + ts]
        gc = cw_ref[0:1] * prev + cw_ref[1:2] * gate[HALO:HALO + ts] + cw_ref[2:3] * nxt + cb_ref[...]
        act = (_silu(gc) * val).astype(BF16)
        part = jnp.dot(act, wd_ref[...], preferred_element_type=F32)
        rows = slice(r * ts, (r + 1) * ts)

        @pl.when(f == 0)
        def _():
            o_ref[0, rows, :] = part

        @pl.when(f > 0)
        def _():
            o_ref[0, rows, :] += part

    @pl.when(f == nf - 1)
    def _():
        x2 = x1_ref[0] + mod_ref[0][5:6] * o_ref[0]
        o_ref[0] = _rms(x2, fnw_ref[...])


def _ffn(h2p, x1, mod, w_gate, w_val, w_down, conv_w, conv_b, final_w, *, tm, ts, tf):
    bsz, seq, d = x1.shape
    dff = w_gate.shape[1]
    return pl.pallas_call(
        functools.partial(_ffn_kernel, tm=tm, ts=ts),
        grid=(bsz, seq // tm, dff // tf),
        in_specs=[
            pl.BlockSpec((1, seq + 2 * HALO, d), lambda b, m, f: (b, 0, 0)),
            pl.BlockSpec((1, tm, d), lambda b, m, f: (b, m, 0)),
            pl.BlockSpec((1, 6, d), lambda b, m, f: (b, 0, 0)),
            pl.BlockSpec((d, tf), lambda b, m, f: (0, f)),
            pl.BlockSpec((d, tf), lambda b, m, f: (0, f)),
            pl.BlockSpec((tf, d), lambda b, m, f: (f, 0)),
            pl.BlockSpec((3, tf), lambda b, m, f: (0, f)),
            pl.BlockSpec((1, tf), lambda b, m, f: (0, f)),
            pl.BlockSpec((1, d), lambda b, m, f: (0, 0)),
        ],
        out_specs=pl.BlockSpec((1, tm, d), lambda b, m, f: (b, m, 0)),
        out_shape=jax.ShapeDtypeStruct((bsz, seq, d), F32),
        compiler_params=_params(("arbitrary", "arbitrary", "arbitrary")),
        name="ffn",
    )(h2p, x1, mod, w_gate, w_val, w_down, conv_w, conv_b.reshape(1, dff), final_w.reshape(1, d))


def kernel(x, c, ctx, c_ctx, ada_w, ada_b, norm1_w, w_in, rpb, ssd_conv_w, ssd_conv_b, dt_bias, a_log,
           ssd_d, ssd_norm_w, w_out, norm2_w, ffn_w_up, ffn_conv_w, ffn_conv_b, ffn_w_down, final_norm_w):
    assert ada_w.shape[0] == 1, "single-layer kernel"
    bsz, seq, d = x.shape
    rows = seq // GRID_W
    nh2 = 2 * SSD_HEADS

    cc = jnp.concatenate([c, c_ctx[None]], axis=0)
    cc = jnp.pad(cc, ((0, (-cc.shape[0]) % 8), (0, 0)))
    mod_all = _ada(cc, ada_w[0], ada_b[0])
    mod = mod_all[:bsz].reshape(bsz, 6, d)
    mod_c = mod_all[bsz:bsz + 1].reshape(1, 6, d)

    w_in_b = w_in[0].astype(BF16)
    col_dt = 3 * NA_WIDTH + SSD_WIDTH + XBC_WIDTH
    w_main = w_in_b[:, :col_dt]
    w_dt = jnp.pad(w_in_b[:, col_dt:], ((0, 0), (0, LANES - nh2)))

    q, k, v, z, xbcp, dt = _inproj(x, mod, norm1_w[0], w_main, w_dt, _rope_tables(seq), latent=True, tm=512)
    kc, vc, xbcp_c, dt_c = _inproj(ctx, mod_c, norm1_w[0], w_main, w_dt, None, latent=False, tm=ctx.shape[1])

    attn = _attention(q, k, v, kc, vc, _bias_tables(rpb[0], rows))

    dtt = jnp.swapaxes(dt[:, :, :nh2], 1, 2)
    dtt_c = jnp.swapaxes(dt_c[:, :, :nh2], 1, 2)
    yssd = _ssd(xbcp, dt, dtt, z, xbcp_c, dt_c, dtt_c, ssd_conv_w[0], ssd_conv_b[0],
                dt_bias[0].reshape(nh2), a_log[0].reshape(nh2), ssd_d[0], ssd_norm_w[0])

    w_out_b = w_out[0].astype(BF16)
    x1, h2p = _outproj(attn, yssd, x, mod, w_out_b[:NA_WIDTH], w_out_b[NA_WIDTH:], norm2_w[0], tm=512)

    w_up_b = ffn_w_up[0].astype(BF16)
    return _ffn(h2p, x1, mod, w_up_b[:, :D_FF], w_up_b[:, D_FF:], ffn_w_down[0].astype(BF16),
                ffn_conv_w[0], ffn_conv_b[0], final_norm_w, tm=1024, ts=512, tf=256)
```

```python
import functools
import math

import numpy as np
import jax
import jax.numpy as jnp
from jax import lax
from jax.experimental import pallas as pl
from jax.experimental.pallas import tpu as pltpu

F32 = jnp.float32
BF16 = jnp.bfloat16

D_MODEL = 1024
GRID_W = 64
EPS = 1e-6
NA_HEADS = 8
HEAD_DIM = 64
NA_WIDTH = NA_HEADS * HEAD_DIM
NA_ROWS = 8
NA_COLS = 16
ROPE_THETA = 10000.0
SSD_HEADS = 8
SSD_HEAD_DIM = 64
SSD_WIDTH = SSD_HEADS * SSD_HEAD_DIM
SSD_GROUPS = 2
SSD_STATE = 128
GN = SSD_GROUPS * SSD_STATE
XBC_WIDTH = SSD_WIDTH + 2 * GN
D_FF = 2816

LANES = 128
CHUNK = 128
HALO = 16
Q_ROWS = 4
BAND_ROWS = Q_ROWS + NA_ROWS - 1
NEG = -1e30
LOG2E = math.log2(math.e)
VMEM_LIMIT = 56 * 1024 * 1024

NT_DIMS = (((1,), (1,)), ((), ()))


def _silu(v):
    return v / (1.0 + jnp.exp(-v))


def _softplus(v):
    return jnp.maximum(v, 0.0) + jnp.log(1.0 + jnp.exp(-jnp.abs(v)))


def _rms(v, w):
    return v * lax.rsqrt(jnp.mean(v * v, axis=-1, keepdims=True) + EPS) * w


def _params(sem):
    return pltpu.CompilerParams(dimension_semantics=sem, vmem_limit_bytes=VMEM_LIMIT)


def _ada_kernel(c_ref, w_ref, b_ref, o_ref):
    a = _silu(c_ref[...])
    o_ref[...] = jnp.dot(a, w_ref[...], preferred_element_type=F32,
                         precision=lax.Precision.HIGHEST) + b_ref[...]


def _ada(cc, ada_w, ada_b):
    rows, d = cc.shape
    n = ada_w.shape[1]
    tn = 1024
    return pl.pallas_call(
        _ada_kernel,
        grid=(n // tn,),
        in_specs=[pl.BlockSpec((rows, d), lambda j: (0, 0)),
                  pl.BlockSpec((d, tn), lambda j: (0, j)),
                  pl.BlockSpec((1, tn), lambda j: (0, j))],
        out_specs=pl.BlockSpec((rows, tn), lambda j: (0, j)),
        out_shape=jax.ShapeDtypeStruct((rows, n), F32),
        compiler_params=_params(("arbitrary",)),
        name="ada",
    )(cc, ada_w, ada_b.reshape(1, n))


def _rope(v, cos, sin_lo, sin_hi):
    outs = []
    for g in range(v.shape[1] // LANES):
        vg = v[:, g * LANES:(g + 1) * LANES]
        outs.append(vg * cos + pltpu.roll(vg, LANES - 32, 1) * sin_lo + pltpu.roll(vg, 32, 1) * sin_hi)
    return jnp.concatenate(outs, axis=1)


def _inproj_kernel(*refs, latent, tm, seq):
    if latent:
        (x_ref, mod_ref, nw_ref, w_ref, wdt_ref, cos_ref, slo_ref, shi_ref,
         q_ref, k_ref, v_ref, z_ref, xbcp_ref, dt_ref) = refs
    else:
        (x_ref, mod_ref, nw_ref, w_ref, wdt_ref, k_ref, v_ref, xbcp_ref, dt_ref) = refs
    m = pl.program_id(1)
    mod = mod_ref[0]
    h = (_rms(x_ref[0], nw_ref[...]) * (1.0 + mod[1:2]) + mod[0:1]).astype(BF16)

    def proj(lo, hi):
        return jnp.dot(h, w_ref[:, lo:hi], preferred_element_type=F32)

    if latent:
        cos, slo, shi = cos_ref[...], slo_ref[...], shi_ref[...]
        q = _rope(proj(0, NA_WIDTH), cos, slo, shi) * (HEAD_DIM ** -0.5 * LOG2E)
        q_ref[0] = q.astype(BF16)
        k_ref[0] = _rope(proj(NA_WIDTH, 2 * NA_WIDTH), cos, slo, shi).astype(BF16)
        z_ref[0] = proj(3 * NA_WIDTH, 3 * NA_WIDTH + SSD_WIDTH).astype(BF16)
    else:
        k_ref[0] = proj(NA_WIDTH, 2 * NA_WIDTH).astype(BF16)
    v_ref[0] = proj(2 * NA_WIDTH, 3 * NA_WIDTH).astype(BF16)

    @pl.when(m == 0)
    def _():
        zeros = jnp.zeros((HALO, XBC_WIDTH), BF16)
        xbcp_ref[0, 0:HALO, :] = zeros
        xbcp_ref[0, seq + HALO:seq + 2 * HALO, :] = zeros

    col = 3 * NA_WIDTH + SSD_WIDTH
    xbcp_ref[0, pl.ds(pl.multiple_of(HALO + m * tm, HALO), tm), :] = proj(col, col + XBC_WIDTH).astype(BF16)
    dt_ref[0] = jnp.dot(h, wdt_ref[...], preferred_element_type=F32)


def _inproj(x, mod, norm_w, w_main, w_dt, tables, *, latent, tm):
    bsz, seq, d = x.shape
    nm = seq // tm
    per_batch_mod = mod.shape[0] == bsz
    kern = functools.partial(_inproj_kernel, latent=latent, tm=tm, seq=seq)
    in_specs = [
        pl.BlockSpec((1, tm, d), lambda b, m: (b, m, 0)),
        pl.BlockSpec((1, 6, d), (lambda b, m: (b, 0, 0)) if per_batch_mod else (lambda b, m: (0, 0, 0))),
        pl.BlockSpec((1, d), lambda b, m: (0, 0)),
        pl.BlockSpec(w_main.shape, lambda b, m: (0, 0)),
        pl.BlockSpec(w_dt.shape, lambda b, m: (0, 0)),
    ]
    args = [x, mod, norm_w.reshape(1, d), w_main, w_dt]
    tile = lambda width, dt: (pl.BlockSpec((1, tm, width), lambda b, m: (b, m, 0)),
                              jax.ShapeDtypeStruct((bsz, seq, width), dt))
    xbcp = (pl.BlockSpec((1, seq + 2 * HALO, XBC_WIDTH), lambda b, m: (b, 0, 0)),
            jax.ShapeDtypeStruct((bsz, seq + 2 * HALO, XBC_WIDTH), BF16))
    if latent:
        in_specs += [pl.BlockSpec((tm, LANES), lambda b, m: (m, 0))] * 3
        args += list(tables)
        outs = [tile(NA_WIDTH, BF16), tile(NA_WIDTH, BF16), tile(NA_WIDTH, BF16), tile(SSD_WIDTH, BF16),
                xbcp, tile(LANES, F32)]
    else:
        outs = [tile(NA_WIDTH, BF16), tile(NA_WIDTH, BF16), xbcp, tile(LANES, F32)]
    return pl.pallas_call(
        kern,
        grid=(bsz, nm),
        in_specs=in_specs,
        out_specs=[o[0] for o in outs],
        out_shape=[o[1] for o in outs],
        compiler_params=_params(("arbitrary", "arbitrary")),
        name="inproj_latent" if latent else "inproj_ctx",
    )(*args)


def _rope_tables(seq):
    t = np.arange(seq)
    row = (t // GRID_W).astype(np.float32)
    col = (t % GRID_W).astype(np.float32)
    n_freq = HEAD_DIM // 4
    freqs = jnp.asarray(ROPE_THETA, F32) ** (-jnp.arange(n_freq, dtype=F32) / n_freq)
    ang = jnp.concatenate([jnp.asarray(row)[:, None] * freqs, jnp.asarray(col)[:, None] * freqs], axis=-1)
    cos, sin = jnp.cos(ang), jnp.sin(ang)
    zero = jnp.zeros_like(sin)
    rep = LANES // HEAD_DIM
    cos_t = jnp.tile(jnp.concatenate([cos, cos], axis=-1), (1, rep))
    sin_lo = jnp.tile(jnp.concatenate([-sin, zero], axis=-1), (1, rep))
    sin_hi = jnp.tile(jnp.concatenate([zero, sin], axis=-1), (1, rep))
    return cos_t, sin_lo, sin_hi


def _bias_tables(rpb, rows):
    nblk = rows // Q_ROWS
    kh = min(NA_ROWS, rows)
    nh, n_dr, n_dc = rpb.shape
    pad = GRID_W - NA_COLS
    padded = jnp.pad(rpb * LOG2E, ((0, 0), (0, 0), (pad, pad)))
    cols = jnp.stack([padded[:, :, GRID_W - 1 - w:2 * GRID_W - 1 - w] for w in range(GRID_W)], axis=2)
    w = np.arange(GRID_W)[:, None]
    c = np.arange(GRID_W)[None, :]
    cs = np.clip(w - NA_COLS // 2, 0, GRID_W - NA_COLS)
    cols = jnp.where((c >= cs) & (c < cs + NA_COLS), cols, NEG)
    cols = jnp.concatenate([cols, jnp.full((nh, 1, GRID_W, GRID_W), NEG, F32)], axis=1)
    qr = np.arange(Q_ROWS)[None, :, None]
    bi = np.arange(BAND_ROWS)[None, None, :]
    kb = np.array((0, 1, nblk - 1))[:, None, None]
    r = Q_ROWS * kb + qr
    rs = np.clip(r - kh // 2, 0, rows - kh)
    krow = np.clip(Q_ROWS * kb - NA_ROWS // 2, 0, rows - BAND_ROWS) + bi
    dr = np.where((krow >= rs) & (krow < rs + kh), krow - r + (NA_ROWS - 1), n_dr)
    blocks = jnp.take(cols, jnp.asarray(dr.reshape(-1)), axis=1)
    blocks = blocks.reshape(nh, 3, Q_ROWS, BAND_ROWS, GRID_W, GRID_W)
    return blocks.transpose(0, 1, 2, 4, 3, 5).reshape(nh, 3, Q_ROWS * GRID_W, BAND_ROWS * GRID_W)


def _attn_kernel(q_ref, k_ref, v_ref, kc_ref, vc_ref, bias_ref, o_ref, *, rows):
    kb = pl.program_id(2)
    nblk = rows // Q_ROWS
    start = jnp.clip(Q_ROWS * kb - NA_ROWS // 2, 0, rows - BAND_ROWS) * GRID_W
    start = pl.multiple_of(start, GRID_W)
    variant = jnp.where(kb == 0, 0, jnp.where(kb == nblk - 1, 2, 1))
    band = BAND_ROWS * GRID_W
    q = q_ref[0]
    kband = k_ref[0, pl.ds(start, band), :]
    vband = v_ref[0, pl.ds(start, band), :]
    kc = kc_ref[0]
    vc = vc_ref[0]
    lane = lax.broadcasted_iota(jnp.int32, (1, LANES), 1)
    out = None
    for hh in range(LANES // HEAD_DIM):
        sel = (lane >= hh * HEAD_DIM) & (lane < (hh + 1) * HEAD_DIM)
        qm = jnp.where(sel, q, jnp.zeros_like(q))
        s_loc = lax.dot_general(qm, kband, NT_DIMS, preferred_element_type=F32) + bias_ref[hh, variant]
        s_ctx = lax.dot_general(qm, kc, NT_DIMS, preferred_element_type=F32)
        mx = jnp.maximum(jnp.max(s_loc, axis=-1, keepdims=True), jnp.max(s_ctx, axis=-1, keepdims=True))
        p_loc = jnp.exp2(s_loc - mx)
        p_ctx = jnp.exp2(s_ctx - mx)
        den = jnp.sum(p_loc, axis=-1, keepdims=True) + jnp.sum(p_ctx, axis=-1, keepdims=True)
        o = (jnp.dot(p_loc.astype(BF16), vband, preferred_element_type=F32)
             + jnp.dot(p_ctx.astype(BF16), vc, preferred_element_type=F32)) / den
        out = o if out is None else jnp.where(sel, o, out)
    o_ref[0] = out.astype(BF16)


def _attention(q, k, v, kc, vc, bias):
    bsz, seq, width = q.shape
    lc = kc.shape[1]
    rows = seq // GRID_W
    nblk = rows // Q_ROWS
    tq = Q_ROWS * GRID_W
    npair = width // LANES
    hpp = LANES // HEAD_DIM
    return pl.pallas_call(
        functools.partial(_attn_kernel, rows=rows),
        grid=(npair, bsz, nblk),
        in_specs=[
            pl.BlockSpec((1, tq, LANES), lambda p, b, kb: (b, kb, p)),
            pl.BlockSpec((1, seq, LANES), lambda p, b, kb: (b, 0, p)),
            pl.BlockSpec((1, seq, LANES), lambda p, b, kb: (b, 0, p)),
            pl.BlockSpec((1, lc, LANES), lambda p, b, kb: (b, 0, p)),
            pl.BlockSpec((1, lc, LANES), lambda p, b, kb: (b, 0, p)),
            pl.BlockSpec((hpp, 3, tq, BAND_ROWS * GRID_W), lambda p, b, kb: (p, 0, 0, 0)),
        ],
        out_specs=pl.BlockSpec((1, tq, LANES), lambda p, b, kb: (b, kb, p)),
        out_shape=jax.ShapeDtypeStruct((bsz, seq, width), BF16),
        compiler_params=_params(("arbitrary", "arbitrary", "arbitrary")),
        name="attn",
    )(q, k, v, kc, vc, bias)


def _ssd_kernel(xbcp_ref, dt_ref, dtt_ref, z_ref, xbcpc_ref, dtc_ref, dttc_ref,
                cw_ref, cb_ref, brow_ref, bcol_ref, arow_ref, acol_ref, dskip_ref, nw_ref,
                y_ref, xact, xactc, yacc, state, *, seq, lc):
    nchunk = seq // CHUNK
    nchunk_c = lc // CHUNK
    win = CHUNK + 2 * HALO

    def conv_pass(src_ref, dst_ref, n):
        def body(c, carry):
            r0 = pl.multiple_of(c * CHUNK, CHUNK)
            w = src_ref[0, pl.ds(r0, win), :].astype(F32)
            prev = pltpu.roll(w, 1, 0)[HALO:HALO + CHUNK]
            nxt = pltpu.roll(w, win - 1, 0)[HALO:HALO + CHUNK]
            g = cw_ref[0:1] * prev + cw_ref[1:2] * w[HALO:HALO + CHUNK] + cw_ref[2:3] * nxt + cb_ref[...]
            dst_ref[pl.ds(r0, CHUNK), :] = _silu(g).astype(BF16)
            return carry
        lax.fori_loop(0, n, body, 0)

    conv_pass(xbcp_ref, xact, nchunk)
    conv_pass(xbcpc_ref, xactc, nchunk_c)
    state[...] = jnp.zeros_like(state)

    a_row = -jnp.exp(arow_ref[...])
    a_col = -jnp.exp(acol_ref[...])
    ii = lax.broadcasted_iota(jnp.int32, (CHUNK, CHUNK), 0)
    jj = lax.broadcasted_iota(jnp.int32, (CHUNK, CHUNK), 1)
    lane_head = lax.broadcasted_iota(jnp.int32, (1, SSD_WIDTH // SSD_GROUPS), 1) // SSD_HEAD_DIM
    hpg = SSD_HEADS // SSD_GROUPS

    def chunk_step(xa, dtr, dtrt, d, need_y):
        dt = _softplus(dtr + brow_ref[...])
        dtt = _softplus(dtrt + bcol_ref[...])
        tri = (jj <= ii) if d == 0 else (jj >= ii)
        tri_f = tri.astype(F32)
        tri_t = ((ii <= jj) if d == 0 else (ii >= jj)).astype(F32)
        cum = jnp.dot(tri_f, dt * a_row, preferred_element_type=F32, precision=lax.Precision.HIGHEST)
        cumt = jnp.dot(dtt * a_col, tri_t, preferred_element_type=F32, precision=lax.Precision.HIGHEST)
        end = CHUNK - 1 if d == 0 else 0
        tot_t = cumt[:, end:end + 1]
        w_t = jnp.exp(tot_t - cumt) * dtt
        cd_t = jnp.exp(tot_t)
        e_in = jnp.exp(cum)
        ys = []
        for g in range(SSD_GROUPS):
            xg = xa[:, g * 256:(g + 1) * 256]
            bg = xa[:, SSD_WIDTH + g * SSD_STATE:SSD_WIDTH + (g + 1) * SSD_STATE]
            cg = xa[:, SSD_WIDTH + GN + g * SSD_STATE:SSD_WIDTH + GN + (g + 1) * SSD_STATE]
            xt = xg.astype(F32).T
            sg = state[d, g]
            if need_y:
                cb = lax.dot_general(cg, bg, NT_DIMS, preferred_element_type=F32)
                rhs_t = jnp.concatenate([xt.astype(BF16), sg.astype(BF16)], axis=1)
                cf = cg.astype(F32)
                yg = None
                for e in range(hpg):
                    hcol = SSD_HEADS * d + hpg * g + e
                    seg = cum[:, hcol:hcol + 1] - cumt[hcol:hcol + 1, :]
                    dec = jnp.exp(jnp.where(tri, seg, NEG))
                    mh = cb * dec * dtt[hcol:hcol + 1, :]
                    ch = cf * e_in[:, hcol:hcol + 1]
                    lhs = jnp.concatenate([mh, ch], axis=1).astype(BF16)
                    yd = lax.dot_general(lhs, rhs_t, NT_DIMS, preferred_element_type=F32)
                    yg = yd if yg is None else jnp.where(lane_head == e, yd, yg)
                ys.append(yg)
            h0 = SSD_HEADS * d + hpg * g
            wexp = jnp.concatenate(
                [jnp.broadcast_to(w_t[h0 + e:h0 + e + 1, :], (SSD_HEAD_DIM, CHUNK)) for e in range(hpg)], axis=0)
            cdexp = jnp.concatenate(
                [jnp.broadcast_to(cd_t[h0 + e:h0 + e + 1, :], (SSD_HEAD_DIM, SSD_STATE)) for e in range(hpg)], axis=0)
            st = jnp.dot((xt * wexp).astype(BF16), bg, preferred_element_type=F32)
            state[d, g] = sg * cdexp + st
        return jnp.concatenate(ys, axis=1) if need_y else None

    def ctx_body(d):
        def body(i, carry):
            c = i if d == 0 else nchunk_c - 1 - i
            r0 = pl.multiple_of(c * CHUNK, CHUNK)
            chunk_step(xactc[pl.ds(r0, CHUNK), :], dtc_ref[0, pl.ds(r0, CHUNK), :],
                       dttc_ref[0, :, pl.ds(r0, CHUNK)], d, False)
            return carry
        return body

    def fwd_body(c, carry):
        r0 = pl.multiple_of(c * CHUNK, CHUNK)
        yacc[pl.ds(r0, CHUNK), :] = chunk_step(
            xact[pl.ds(r0, CHUNK), :], dt_ref[0, pl.ds(r0, CHUNK), :], dtt_ref[0, :, pl.ds(r0, CHUNK)], 0, True)
        return carry

    def bwd_body(i, carry):
        c = nchunk - 1 - i
        r0 = pl.multiple_of(c * CHUNK, CHUNK)
        xa = xact[pl.ds(r0, CHUNK), :]
        yb = chunk_step(xa, dt_ref[0, pl.ds(r0, CHUNK), :], dtt_ref[0, :, pl.ds(r0, CHUNK)], 1, True)
        y = yacc[pl.ds(r0, CHUNK), :] + yb + dskip_ref[...] * xa[:, :SSD_WIDTH].astype(F32)
        y = y * _silu(z_ref[0, pl.ds(r0, CHUNK), :].astype(F32))
        y_ref[0, pl.ds(r0, CHUNK), :] = _rms(y, nw_ref[...]).astype(BF16)
        return carry

    lax.fori_loop(0, nchunk_c, ctx_body(0), 0)
    lax.fori_loop(0, nchunk, fwd_body, 0)
    lax.fori_loop(0, nchunk_c, ctx_body(1), 0)
    lax.fori_loop(0, nchunk, bwd_body, 0)


def _ssd(xbcp, dt, dtt, z, xbcp_c, dt_c, dtt_c, conv_w, conv_b, dt_bias, a_log, d_skip, norm_w):
    bsz, seq, _ = z.shape
    lc = dt_c.shape[1]
    nh2 = 2 * SSD_HEADS
    pad_row = lambda v: jnp.pad(v.reshape(1, nh2), ((0, 0), (0, LANES - nh2)))
    full = lambda shape: pl.BlockSpec(shape, lambda b: (0,) * len(shape))
    per_b = lambda shape: pl.BlockSpec((1,) + shape, lambda b: (b, 0, 0))
    return pl.pallas_call(
        functools.partial(_ssd_kernel, seq=seq, lc=lc),
        grid=(bsz,),
        in_specs=[
            per_b((seq + 2 * HALO, XBC_WIDTH)), per_b((seq, LANES)), per_b((nh2, seq)), per_b((seq, SSD_WIDTH)),
            per_b((lc + 2 * HALO, XBC_WIDTH)), per_b((lc, LANES)), per_b((nh2, lc)),
            full((3, XBC_WIDTH)), full((1, XBC_WIDTH)),
            full((1, LANES)), full((nh2, 1)), full((1, LANES)), full((nh2, 1)),
            full((1, SSD_WIDTH)), full((1, SSD_WIDTH)),
        ],
        out_specs=per_b((seq, SSD_WIDTH)),
        out_shape=jax.ShapeDtypeStruct((bsz, seq, SSD_WIDTH), BF16),
        scratch_shapes=[
            pltpu.VMEM((seq, XBC_WIDTH), BF16),
            pltpu.VMEM((lc, XBC_WIDTH), BF16),
            pltpu.VMEM((seq, SSD_WIDTH), F32),
            pltpu.VMEM((2, SSD_GROUPS, SSD_WIDTH // SSD_GROUPS, SSD_STATE), F32),
        ],
        compiler_params=_params(("arbitrary",)),
        name="ssd",
    )(xbcp, dt, dtt, z, xbcp_c, dt_c, dtt_c, conv_w, conv_b.reshape(1, XBC_WIDTH),
      pad_row(dt_bias), dt_bias.reshape(nh2, 1), pad_row(a_log), a_log.reshape(nh2, 1),
      jnp.repeat(d_skip, SSD_HEAD_DIM).reshape(1, SSD_WIDTH), norm_w.reshape(1, SSD_WIDTH))


def _outproj_kernel(attn_ref, yssd_ref, x_ref, mod_ref, wa_ref, ws_ref, nw_ref, x1_ref, h2p_ref, *, tm, seq):
    m = pl.program_id(1)
    mod = mod_ref[0]
    mix = (jnp.dot(attn_ref[0], wa_ref[...], preferred_element_type=F32)
           + jnp.dot(yssd_ref[0], ws_ref[...], preferred_element_type=F32))
    x1 = x_ref[0] + mod[2:3] * mix
    x1_ref[0] = x1
    h2 = _rms(x1, nw_ref[...]) * (1.0 + mod[4:5]) + mod[3:4]

    @pl.when(m == 0)
    def _():
        zeros = jnp.zeros((HALO, D_MODEL), BF16)
        h2p_ref[0, 0:HALO, :] = zeros
        h2p_ref[0, seq + HALO:seq + 2 * HALO, :] = zeros

    h2p_ref[0, pl.ds(pl.multiple_of(HALO + m * tm, HALO), tm), :] = h2.astype(BF16)


def _outproj(attn, yssd, x, mod, w_attn, w_ssd, norm_w, *, tm):
    bsz, seq, d = x.shape
    return pl.pallas_call(
        functools.partial(_outproj_kernel, tm=tm, seq=seq),
        grid=(bsz, seq // tm),
        in_specs=[
            pl.BlockSpec((1, tm, NA_WIDTH), lambda b, m: (b, m, 0)),
            pl.BlockSpec((1, tm, SSD_WIDTH), lambda b, m: (b, m, 0)),
            pl.BlockSpec((1, tm, d), lambda b, m: (b, m, 0)),
            pl.BlockSpec((1, 6, d), lambda b, m: (b, 0, 0)),
            pl.BlockSpec(w_attn.shape, lambda b, m: (0, 0)),
            pl.BlockSpec(w_ssd.shape, lambda b, m: (0, 0)),
            pl.BlockSpec((1, d), lambda b, m: (0, 0)),
        ],
        out_specs=[pl.BlockSpec((1, tm, d), lambda b, m: (b, m, 0)),
                   pl.BlockSpec((1, seq + 2 * HALO, d), lambda b, m: (b, 0, 0))],
        out_shape=[jax.ShapeDtypeStruct((bsz, seq, d), F32),
                   jax.ShapeDtypeStruct((bsz, seq + 2 * HALO, d), BF16)],
        compiler_params=_params(("arbitrary", "arbitrary")),
        name="outproj",
    )(attn, yssd, x, mod, w_attn, w_ssd, norm_w.reshape(1, d))


def _ffn_kernel(h2p_ref, x1_ref, mod_ref, wg_ref, wv_ref, wd_ref, cw_ref, cb_ref, fnw_ref, o_ref, *, tm, ts):
    m = pl.program_id(1)
    f = pl.program_id(2)
    nf = pl.num_programs(2)
    win = ts + 2 * HALO
    for r in range(tm // ts):
        start = pl.multiple_of(m * tm + r * ts, ts)
        hb = h2p_ref[0, pl.ds(start, win), :]
        gate = jnp.dot(hb, wg_ref[...], preferred_element_type=F32)
        val = jnp.dot(hb[HALO:HALO + ts], wv_ref[...], preferred_element_type=F32)
        prev = pltpu.roll(gate, 1, 0)[HALO:HALO + ts]
        nxt = pltpu.roll(gate, win - 1, 0)[HALO:HALO + ts]
        gc = cw_ref[0:1] * prev + cw_ref[1:2] * gate[HALO:HALO + ts] + cw_ref[2:3] * nxt + cb_ref[...]
        act = (_silu(gc) * val).astype(BF16)
        part = jnp.dot(act, wd_ref[...], preferred_element_type=F32)
        rows = slice(r * ts, (r + 1) * ts)

        @pl.when(f == 0)
        def _():
            o_ref[0, rows, :] = part

        @pl.when(f > 0)
        def _():
            o_ref[0, rows, :] += part

    @pl.when(f == nf - 1)
    def _():
        x2 = x1_ref[0] + mod_ref[0][5:6] * o_ref[0]
        o_ref[0] = _rms(x2, fnw_ref[...])


def _ffn(h2p, x1, mod, w_gate, w_val, w_down, conv_w, conv_b, final_w, *, tm, ts, tf):
    bsz, seq, d = x1.shape
    dff = w_gate.shape[1]
    return pl.pallas_call(
        functools.partial(_ffn_kernel, tm=tm, ts=ts),
        grid=(bsz, seq // tm, dff // tf),
        in_specs=[
            pl.BlockSpec((1, seq + 2 * HALO, d), lambda b, m, f: (b, 0, 0)),
            pl.BlockSpec((1, tm, d), lambda b, m, f: (b, m, 0)),
            pl.BlockSpec((1, 6, d), lambda b, m, f: (b, 0, 0)),
            pl.BlockSpec((d, tf), lambda b, m, f: (0, f)),
            pl.BlockSpec((d, tf), lambda b, m, f: (0, f)),
            pl.BlockSpec((tf, d), lambda b, m, f: (f, 0)),
            pl.BlockSpec((3, tf), lambda b, m, f: (0, f)),
            pl.BlockSpec((1, tf), lambda b, m, f: (0, f)),
            pl.BlockSpec((1, d), lambda b, m, f: (0, 0)),
        ],
        out_specs=pl.BlockSpec((1, tm, d), lambda b, m, f: (b, m, 0)),
        out_shape=jax.ShapeDtypeStruct((bsz, seq, d), F32),
        compiler_params=_params(("arbitrary", "arbitrary", "arbitrary")),
        name="ffn",
    )(h2p, x1, mod, w_gate, w_val, w_down, conv_w, conv_b.reshape(1, dff), final_w.reshape(1, d))


def kernel(x, c, ctx, c_ctx, ada_w, ada_b, norm1_w, w_in, rpb, ssd_conv_w, ssd_conv_b, dt_bias, a_log,
           ssd_d, ssd_norm_w, w_out, norm2_w, ffn_w_up, ffn_conv_w, ffn_conv_b, ffn_w_down, final_norm_w):
    assert ada_w.shape[0] == 1, "single-layer kernel"
    bsz, seq, d = x.shape
    rows = seq // GRID_W
    nh2 = 2 * SSD_HEADS

    cc = jnp.concatenate([c, c_ctx[None]], axis=0)
    cc = jnp.pad(cc, ((0, (-cc.shape[0]) % 8), (0, 0)))
    mod_all = _ada(cc, ada_w[0], ada_b[0])
    mod = mod_all[:bsz].reshape(bsz, 6, d)
    mod_c = mod_all[bsz:bsz + 1].reshape(1, 6, d)

    w_in_b = w_in[0].astype(BF16)
    col_dt = 3 * NA_WIDTH + SSD_WIDTH + XBC_WIDTH
    w_main = w_in_b[:, :col_dt]
    w_dt = jnp.pad(w_in_b[:, col_dt:], ((0, 0), (0, LANES - nh2)))

    q, k, v, z, xbcp, dt = _inproj(x, mod, norm1_w[0], w_main, w_dt, _rope_tables(seq), latent=True, tm=512)
    kc, vc, xbcp_c, dt_c = _inproj(ctx, mod_c, norm1_w[0], w_main, w_dt, None, latent=False, tm=ctx.shape[1])

    attn = _attention(q, k, v, kc, vc, _bias_tables(rpb[0], rows))

    dtt = jnp.swapaxes(dt[:, :, :nh2], 1, 2)
    dtt_c = jnp.swapaxes(dt_c[:, :, :nh2], 1, 2)
    yssd = _ssd(xbcp, dt, dtt, z, xbcp_c, dt_c, dtt_c, ssd_conv_w[0], ssd_conv_b[0],
                dt_bias[0].reshape(nh2), a_log[0].reshape(nh2), ssd_d[0], ssd_norm_w[0])

    w_out_b = w_out[0].astype(BF16)
    x1, h2p = _outproj(attn, yssd, x, mod, w_out_b[:NA_WIDTH], w_out_b[NA_WIDTH:], norm2_w[0], tm=512)

    w_up_b = ffn_w_up[0].astype(BF16)
    return _ffn(h2p, x1, mod, w_up_b[:, :D_FF], w_up_b[:, D_FF:], ffn_w_down[0].astype(BF16),
                ffn_conv_w[0], ffn_conv_b[0], final_norm_w, tm=1024, ts=512, tf=256)
```

```python
import functools
import math

import numpy as np
import jax
import jax.numpy as jnp
from jax import lax
from jax.experimental import pallas as pl
from jax.experimental.pallas import tpu as pltpu

F32 = jnp.float32
BF16 = jnp.bfloat16

D_MODEL = 1024
GRID_W = 64
EPS = 1e-6
NA_HEADS = 8
HEAD_DIM = 64
NA_WIDTH = NA_HEADS * HEAD_DIM
NA_ROWS = 8
NA_COLS = 16
ROPE_THETA = 10000.0
SSD_HEADS = 8
SSD_HEAD_DIM = 64
SSD_WIDTH = SSD_HEADS * SSD_HEAD_DIM
SSD_GROUPS = 2
SSD_STATE = 128
GN = SSD_GROUPS * SSD_STATE
XBC_WIDTH = SSD_WIDTH + 2 * GN
D_FF = 2816

LANES = 128
CHUNK = 128
HALO = 16
Q_ROWS = 4
BAND_ROWS = Q_ROWS + NA_ROWS
NEG = -1e30
LOG2E = math.log2(math.e)
VMEM_LIMIT = 56 * 1024 * 1024

NT_DIMS = (((1,), (1,)), ((), ()))


def _silu(v):
    return v / (1.0 + jnp.exp(-v))


def _softplus(v):
    return jnp.maximum(v, 0.0) + jnp.log(1.0 + jnp.exp(-jnp.abs(v)))


def _rms(v, w):
    return v * lax.rsqrt(jnp.mean(v * v, axis=-1, keepdims=True) + EPS) * w


def _params(sem):
    return pltpu.CompilerParams(dimension_semantics=sem, vmem_limit_bytes=VMEM_LIMIT)


def _ada_kernel(c_ref, w_ref, b_ref, o_ref):
    a = _silu(c_ref[...])
    o_ref[...] = jnp.dot(a, w_ref[...], preferred_element_type=F32,
                         precision=lax.Precision.HIGHEST) + b_ref[...]


def _ada(cc, ada_w, ada_b):
    rows, d = cc.shape
    n = ada_w.shape[1]
    tn = 1024
    return pl.pallas_call(
        _ada_kernel,
        grid=(n // tn,),
        in_specs=[pl.BlockSpec((rows, d), lambda j: (0, 0)),
                  pl.BlockSpec((d, tn), lambda j: (0, j)),
                  pl.BlockSpec((1, tn), lambda j: (0, j))],
        out_specs=pl.BlockSpec((rows, tn), lambda j: (0, j)),
        out_shape=jax.ShapeDtypeStruct((rows, n), F32),
        compiler_params=_params(("arbitrary",)),
        name="ada",
    )(cc, ada_w, ada_b.reshape(1, n))


def _rope(v, cos, sin_lo, sin_hi):
    outs = []
    for g in range(v.shape[1] // LANES):
        vg = v[:, g * LANES:(g + 1) * LANES]
        outs.append(vg * cos + pltpu.roll(vg, LANES - 32, 1) * sin_lo + pltpu.roll(vg, 32, 1) * sin_hi)
    return jnp.concatenate(outs, axis=1)


def _inproj_kernel(*refs, latent, tm, seq):
    if latent:
        (x_ref, mod_ref, nw_ref, w_ref, wdt_ref, cos_ref, slo_ref, shi_ref,
         q_ref, k_ref, vt_ref, z_ref, xbcp_ref, dt_ref) = refs
    else:
        (x_ref, mod_ref, nw_ref, w_ref, wdt_ref, k_ref, vt_ref, xbcp_ref, dt_ref) = refs
    m = pl.program_id(1)
    mod = mod_ref[0]
    h = (_rms(x_ref[0], nw_ref[...]) * (1.0 + mod[1:2]) + mod[0:1]).astype(BF16)

    def proj(lo, hi):
        return jnp.dot(h, w_ref[:, lo:hi], preferred_element_type=F32)

    if latent:
        cos, slo, shi = cos_ref[...], slo_ref[...], shi_ref[...]
        q = _rope(proj(0, NA_WIDTH), cos, slo, shi) * (HEAD_DIM ** -0.5 * LOG2E)
        q_ref[0] = q.astype(BF16)
        k_ref[0] = _rope(proj(NA_WIDTH, 2 * NA_WIDTH), cos, slo, shi).astype(BF16)
        z_ref[0] = proj(3 * NA_WIDTH, 3 * NA_WIDTH + SSD_WIDTH).astype(BF16)
    else:
        k_ref[0] = proj(NA_WIDTH, 2 * NA_WIDTH).astype(BF16)
    vt_ref[0] = proj(2 * NA_WIDTH, 3 * NA_WIDTH).T.astype(BF16)

    @pl.when(m == 0)
    def _():
        zeros = jnp.zeros((HALO, XBC_WIDTH), BF16)
        xbcp_ref[0, 0:HALO, :] = zeros
        xbcp_ref[0, seq + HALO:seq + 2 * HALO, :] = zeros

    col = 3 * NA_WIDTH + SSD_WIDTH
    xbcp_ref[0, pl.ds(pl.multiple_of(HALO + m * tm, HALO), tm), :] = proj(col, col + XBC_WIDTH).astype(BF16)
    dt_ref[0] = jnp.dot(h, wdt_ref[...], preferred_element_type=F32)


def _inproj(x, mod, norm_w, w_main, w_dt, tables, *, latent, tm):
    bsz, seq, d = x.shape
    nm = seq // tm
    per_batch_mod = mod.shape[0] == bsz
    kern = functools.partial(_inproj_kernel, latent=latent, tm=tm, seq=seq)
    in_specs = [
        pl.BlockSpec((1, tm, d), lambda b, m: (b, m, 0)),
        pl.BlockSpec((1, 6, d), (lambda b, m: (b, 0, 0)) if per_batch_mod else (lambda b, m: (0, 0, 0))),
        pl.BlockSpec((1, d), lambda b, m: (0, 0)),
        pl.BlockSpec(w_main.shape, lambda b, m: (0, 0)),
        pl.BlockSpec(w_dt.shape, lambda b, m: (0, 0)),
    ]
    args = [x, mod, norm_w.reshape(1, d), w_main, w_dt]
    tile = lambda width, dt: (pl.BlockSpec((1, tm, width), lambda b, m: (b, m, 0)),
                              jax.ShapeDtypeStruct((bsz, seq, width), dt))
    xbcp = (pl.BlockSpec((1, seq + 2 * HALO, XBC_WIDTH), lambda b, m: (b, 0, 0)),
            jax.ShapeDtypeStruct((bsz, seq + 2 * HALO, XBC_WIDTH), BF16))
    vt = (pl.BlockSpec((1, NA_WIDTH, tm), lambda b, m: (b, 0, m)),
          jax.ShapeDtypeStruct((bsz, NA_WIDTH, seq), BF16))
    if latent:
        in_specs += [pl.BlockSpec((tm, LANES), lambda b, m: (m, 0))] * 3
        args += list(tables)
        outs = [tile(NA_WIDTH, BF16), tile(NA_WIDTH, BF16), vt, tile(SSD_WIDTH, BF16), xbcp, tile(LANES, F32)]
    else:
        outs = [tile(NA_WIDTH, BF16), vt, xbcp, tile(LANES, F32)]
    return pl.pallas_call(
        kern,
        grid=(bsz, nm),
        in_specs=in_specs,
        out_specs=[o[0] for o in outs],
        out_shape=[o[1] for o in outs],
        compiler_params=_params(("arbitrary", "arbitrary")),
        name="inproj_latent" if latent else "inproj_ctx",
    )(*args)


def _rope_tables(seq):
    t = np.arange(seq)
    row = (t // GRID_W).astype(np.float32)
    col = (t % GRID_W).astype(np.float32)
    n_freq = HEAD_DIM // 4
    freqs = jnp.asarray(ROPE_THETA, F32) ** (-jnp.arange(n_freq, dtype=F32) / n_freq)
    ang = jnp.concatenate([jnp.asarray(row)[:, None] * freqs, jnp.asarray(col)[:, None] * freqs], axis=-1)
    cos, sin = jnp.cos(ang), jnp.sin(ang)
    zero = jnp.zeros_like(sin)
    rep = LANES // HEAD_DIM
    cos_t = jnp.tile(jnp.concatenate([cos, cos], axis=-1), (1, rep))
    sin_lo = jnp.tile(jnp.concatenate([-sin, zero], axis=-1), (1, rep))
    sin_hi = jnp.tile(jnp.concatenate([zero, sin], axis=-1), (1, rep))
    return cos_t, sin_lo, sin_hi


def _bias_tables(rpb, rows):
    nblk = rows // Q_ROWS
    kh = min(NA_ROWS, rows)
    nh = rpb.shape[0]
    pad = GRID_W - NA_COLS
    flipped = jnp.flip(jnp.pad(rpb * LOG2E, ((0, 0), (0, 0), (pad, pad))), axis=-1)
    cols = jnp.stack([flipped[:, :, GRID_W - 1 - c:2 * GRID_W - 1 - c] for c in range(GRID_W)], axis=2)
    c = np.arange(GRID_W)[:, None]
    w = np.arange(GRID_W)[None, :]
    cs = np.clip(w - NA_COLS // 2, 0, GRID_W - NA_COLS)
    cols = jnp.where((c >= cs) & (c < cs + NA_COLS), cols, NEG)
    masked = jnp.full((nh, GRID_W, GRID_W), NEG, F32)
    variants = []
    for kb in (0, 1, nblk - 1):
        band_rows = []
        for bi in range(BAND_ROWS):
            blocks = []
            for qr in range(Q_ROWS):
                r = Q_ROWS * kb + qr
                rs = min(max(r - kh // 2, 0), rows - kh)
                krow = _band_start(kb, rows) + bi
                blocks.append(cols[:, krow - r + NA_ROWS - 1] if rs <= krow < rs + kh else masked)
            band_rows.append(jnp.concatenate(blocks, axis=-1))
        variants.append(jnp.concatenate(band_rows, axis=-2))
    return jnp.stack(variants, axis=1)


def _band_start(kb, rows):
    return min(max(Q_ROWS * kb - NA_ROWS // 2, 0), rows - BAND_ROWS)


def _attn_kernel(q_ref, k_ref, vt_ref, kc_ref, vct_ref, bias_ref, o_ref, s_scr, p_scr, *, rows, qblocks):
    nblk = rows // Q_ROWS
    tq = Q_ROWS * GRID_W
    band = BAND_ROWS * GRID_W
    kc = kc_ref[0]
    vct = vct_ref[0]
    lane = lax.broadcasted_iota(jnp.int32, (1, LANES), 1)
    hd = HEAD_DIM
    for t in range(qblocks):
        kb = pl.program_id(2) * qblocks + t
        start = jnp.clip(Q_ROWS * kb - NA_ROWS // 2, 0, rows - BAND_ROWS) * GRID_W
        start = pl.multiple_of(start, Q_ROWS * GRID_W)
        variant = jnp.where(kb == 0, 0, jnp.where(kb == nblk - 1, 2, 1))
        q = q_ref[0, t * tq:(t + 1) * tq, :]
        kband = k_ref[0, pl.ds(start, band), :]
        vtband = vt_ref[0, :, pl.ds(start, band)]
        nhp = LANES // hd
        qm = jnp.concatenate(
            [jnp.where((lane >= hh * hd) & (lane < (hh + 1) * hd), q, jnp.zeros_like(q)) for hh in range(nhp)], axis=0)
        bias = jnp.concatenate([bias_ref[hh, variant] for hh in range(nhp)], axis=1)
        s_ref, p_ref = s_scr.at[t % 2], p_scr.at[t % 2]
        s_ref[0:band, :] = lax.dot_general(kband, qm, NT_DIMS, preferred_element_type=F32) + bias
        s_ref[band:, :] = lax.dot_general(kc, qm, NT_DIMS, preferred_element_type=F32)
        s = s_ref[...]
        p = jnp.exp2(s - jnp.max(s, axis=0, keepdims=True))
        den = jnp.sum(p, axis=0, keepdims=True)
        p_ref[...] = p.astype(BF16)
        ot = (jnp.dot(vtband, p_ref[0:band, :], preferred_element_type=F32)
              + jnp.dot(vct, p_ref[band:, :], preferred_element_type=F32)) / den
        out_t = jnp.concatenate([ot[hh * hd:(hh + 1) * hd, hh * tq:(hh + 1) * tq] for hh in range(nhp)], axis=0)
        o_ref[0, t * tq:(t + 1) * tq, :] = out_t.T.astype(BF16)


def _attention(q, k, vt, kc, vct, bias, *, qblocks):
    bsz, seq, width = q.shape
    lc = kc.shape[1]
    rows = seq // GRID_W
    nblk = rows // Q_ROWS
    tq = Q_ROWS * GRID_W * qblocks
    npair = width // LANES
    hpp = LANES // HEAD_DIM
    return pl.pallas_call(
        functools.partial(_attn_kernel, rows=rows, qblocks=qblocks),
        grid=(npair, bsz, nblk // qblocks),
        in_specs=[
            pl.BlockSpec((1, tq, LANES), lambda p, b, kb: (b, kb, p)),
            pl.BlockSpec((1, seq, LANES), lambda p, b, kb: (b, 0, p)),
            pl.BlockSpec((1, LANES, seq), lambda p, b, kb: (b, p, 0)),
            pl.BlockSpec((1, lc, LANES), lambda p, b, kb: (b, 0, p)),
            pl.BlockSpec((1, LANES, lc), lambda p, b, kb: (b, p, 0)),
            pl.BlockSpec((hpp, 3, BAND_ROWS * GRID_W, Q_ROWS * GRID_W), lambda p, b, kb: (p, 0, 0, 0)),
        ],
        out_specs=pl.BlockSpec((1, tq, LANES), lambda p, b, kb: (b, kb, p)),
        out_shape=jax.ShapeDtypeStruct((bsz, seq, width), BF16),
        scratch_shapes=[pltpu.VMEM((2, BAND_ROWS * GRID_W + lc, hpp * Q_ROWS * GRID_W), F32),
                        pltpu.VMEM((2, BAND_ROWS * GRID_W + lc, hpp * Q_ROWS * GRID_W), BF16)],
        compiler_params=_params(("arbitrary", "arbitrary", "arbitrary")),
        name="attn",
    )(q, k, vt, kc, vct, bias)


def _ssd_kernel(xbcp_ref, dt_ref, dtt_ref, z_ref, xbcpc_ref, dtc_ref, dttc_ref,
                cw_ref, cb_ref, brow_ref, bcol_ref, arow_ref, acol_ref, dskip_ref, nw_ref,
                y_ref, xact, xactc, yacc, state, *, seq, lc):
    nchunk = seq // CHUNK
    nchunk_c = lc // CHUNK
    win = CHUNK + 2 * HALO

    def conv_pass(src_ref, dst_ref, n):
        def body(c, carry):
            r0 = pl.multiple_of(c * CHUNK, CHUNK)
            w = src_ref[0, pl.ds(r0, win), :].astype(F32)
            prev = pltpu.roll(w, 1, 0)[HALO:HALO + CHUNK]
            nxt = pltpu.roll(w, win - 1, 0)[HALO:HALO + CHUNK]
            g = cw_ref[0:1] * prev + cw_ref[1:2] * w[HALO:HALO + CHUNK] + cw_ref[2:3] * nxt + cb_ref[...]
            dst_ref[pl.ds(r0, CHUNK), :] = _silu(g).astype(BF16)
            return carry
        lax.fori_loop(0, n, body, 0)

    conv_pass(xbcp_ref, xact, nchunk)
    conv_pass(xbcpc_ref, xactc, nchunk_c)
    state[...] = jnp.zeros_like(state)

    a_row = -jnp.exp(arow_ref[...])
    a_col = -jnp.exp(acol_ref[...])
    ii = lax.broadcasted_iota(jnp.int32, (CHUNK, CHUNK), 0)
    jj = lax.broadcasted_iota(jnp.int32, (CHUNK, CHUNK), 1)
    lane_head = lax.broadcasted_iota(jnp.int32, (1, SSD_WIDTH // SSD_GROUPS), 1) // SSD_HEAD_DIM
    hpg = SSD_HEADS // SSD_GROUPS

    def chunk_step(xa, dtr, dtrt, d, need_y):
        dt = _softplus(dtr + brow_ref[...])
        dtt = _softplus(dtrt + bcol_ref[...])
        tri = (jj <= ii) if d == 0 else (jj >= ii)
        tri_f = tri.astype(F32)
        tri_t = ((ii <= jj) if d == 0 else (ii >= jj)).astype(F32)
        cum = jnp.dot(tri_f, dt * a_row, preferred_element_type=F32, precision=lax.Precision.HIGHEST)
        cumt = jnp.dot(dtt * a_col, tri_t, preferred_element_type=F32, precision=lax.Precision.HIGHEST)
        end = CHUNK - 1 if d == 0 else 0
        tot_t = cumt[:, end:end + 1]
        w_t = jnp.exp(tot_t - cumt) * dtt
        cd_t = jnp.exp(tot_t)
        e_in = jnp.exp(cum)
        ys = []
        for g in range(SSD_GROUPS):
            xg = xa[:, g * 256:(g + 1) * 256]
            bg = xa[:, SSD_WIDTH + g * SSD_STATE:SSD_WIDTH + (g + 1) * SSD_STATE]
            cg = xa[:, SSD_WIDTH + GN + g * SSD_STATE:SSD_WIDTH + GN + (g + 1) * SSD_STATE]
            xt = xg.astype(F32).T
            sg = state[d, g]
            if need_y:
                cb = lax.dot_general(cg, bg, NT_DIMS, preferred_element_type=F32)
                rhs_t = jnp.concatenate([xt.astype(BF16), sg.astype(BF16)], axis=1)
                cf = cg.astype(F32)
                yg = None
                for e in range(hpg):
                    hcol = SSD_HEADS * d + hpg * g + e
                    seg = cum[:, hcol:hcol + 1] - cumt[hcol:hcol + 1, :]
                    dec = jnp.exp(jnp.where(tri, seg, NEG))
                    mh = cb * dec * dtt[hcol:hcol + 1, :]
                    ch = cf * e_in[:, hcol:hcol + 1]
                    lhs = jnp.concatenate([mh, ch], axis=1).astype(BF16)
                    yd = lax.dot_general(lhs, rhs_t, NT_DIMS, preferred_element_type=F32)
                    yg = yd if yg is None else jnp.where(lane_head == e, yd, yg)
                ys.append(yg)
            h0 = SSD_HEADS * d + hpg * g
            wexp = jnp.concatenate(
                [jnp.broadcast_to(w_t[h0 + e:h0 + e + 1, :], (SSD_HEAD_DIM, CHUNK)) for e in range(hpg)], axis=0)
            cdexp = jnp.concatenate(
                [jnp.broadcast_to(cd_t[h0 + e:h0 + e + 1, :], (SSD_HEAD_DIM, SSD_STATE)) for e in range(hpg)], axis=0)
            st = jnp.dot((xt * wexp).astype(BF16), bg, preferred_element_type=F32)
            state[d, g] = sg * cdexp + st
        return jnp.concatenate(ys, axis=1) if need_y else None

    def ctx_body(d):
        def body(i, carry):
            c = i if d == 0 else nchunk_c - 1 - i
            r0 = pl.multiple_of(c * CHUNK, CHUNK)
            chunk_step(xactc[pl.ds(r0, CHUNK), :], dtc_ref[0, pl.ds(r0, CHUNK), :],
                       dttc_ref[0, :, pl.ds(r0, CHUNK)], d, False)
            return carry
        return body

    def fwd_body(c, carry):
        r0 = pl.multiple_of(c * CHUNK, CHUNK)
        yacc[pl.ds(r0, CHUNK), :] = chunk_step(
            xact[pl.ds(r0, CHUNK), :], dt_ref[0, pl.ds(r0, CHUNK), :], dtt_ref[0, :, pl.ds(r0, CHUNK)], 0, True)
        return carry

    def bwd_body(i, carry):
        c = nchunk - 1 - i
        r0 = pl.multiple_of(c * CHUNK, CHUNK)
        xa = xact[pl.ds(r0, CHUNK), :]
        yb = chunk_step(xa, dt_ref[0, pl.ds(r0, CHUNK), :], dtt_ref[0, :, pl.ds(r0, CHUNK)], 1, True)
        y = yacc[pl.ds(r0, CHUNK), :] + yb + dskip_ref[...] * xa[:, :SSD_WIDTH].astype(F32)
        y = y * _silu(z_ref[0, pl.ds(r0, CHUNK), :].astype(F32))
        y_ref[0, pl.ds(r0, CHUNK), :] = _rms(y, nw_ref[...]).astype(BF16)
        return carry

    lax.fori_loop(0, nchunk_c, ctx_body(0), 0)
    lax.fori_loop(0, nchunk, fwd_body, 0)
    lax.fori_loop(0, nchunk_c, ctx_body(1), 0)
    lax.fori_loop(0, nchunk, bwd_body, 0)


def _ssd(xbcp, dt, dtt, z, xbcp_c, dt_c, dtt_c, conv_w, conv_b, dt_bias, a_log, d_skip, norm_w):
    bsz, seq, _ = z.shape
    lc = dt_c.shape[1]
    nh2 = 2 * SSD_HEADS
    pad_row = lambda v: jnp.pad(v.reshape(1, nh2), ((0, 0), (0, LANES - nh2)))
    full = lambda shape: pl.BlockSpec(shape, lambda b: (0,) * len(shape))
    per_b = lambda shape: pl.BlockSpec((1,) + shape, lambda b: (b, 0, 0))
    return pl.pallas_call(
        functools.partial(_ssd_kernel, seq=seq, lc=lc),
        grid=(bsz,),
        in_specs=[
            per_b((seq + 2 * HALO, XBC_WIDTH)), per_b((seq, LANES)), per_b((nh2, seq)), per_b((seq, SSD_WIDTH)),
            per_b((lc + 2 * HALO, XBC_WIDTH)), per_b((lc, LANES)), per_b((nh2, lc)),
            full((3, XBC_WIDTH)), full((1, XBC_WIDTH)),
            full((1, LANES)), full((nh2, 1)), full((1, LANES)), full((nh2, 1)),
            full((1, SSD_WIDTH)), full((1, SSD_WIDTH)),
        ],
        out_specs=per_b((seq, SSD_WIDTH)),
        out_shape=jax.ShapeDtypeStruct((bsz, seq, SSD_WIDTH), BF16),
        scratch_shapes=[
            pltpu.VMEM((seq, XBC_WIDTH), BF16),
            pltpu.VMEM((lc, XBC_WIDTH), BF16),
            pltpu.VMEM((seq, SSD_WIDTH), F32),
            pltpu.VMEM((2, SSD_GROUPS, SSD_WIDTH // SSD_GROUPS, SSD_STATE), F32),
        ],
        compiler_params=_params(("arbitrary",)),
        name="ssd",
    )(xbcp, dt, dtt, z, xbcp_c, dt_c, dtt_c, conv_w, conv_b.reshape(1, XBC_WIDTH),
      pad_row(dt_bias), dt_bias.reshape(nh2, 1), pad_row(a_log), a_log.reshape(nh2, 1),
      jnp.repeat(d_skip, SSD_HEAD_DIM).reshape(1, SSD_WIDTH), norm_w.reshape(1, SSD_WIDTH))


def _outproj_kernel(attn_ref, yssd_ref, x_ref, mod_ref, wa_ref, ws_ref, nw_ref, x1_ref, h2p_ref, *, tm, seq):
    m = pl.program_id(1)
    mod = mod_ref[0]
    mix = (jnp.dot(attn_ref[0], wa_ref[...], preferred_element_type=F32)
           + jnp.dot(yssd_ref[0], ws_ref[...], preferred_element_type=F32))
    x1 = x_ref[0] + mod[2:3] * mix
    x1_ref[0] = x1
    h2 = _rms(x1, nw_ref[...]) * (1.0 + mod[4:5]) + mod[3:4]

    @pl.when(m == 0)
    def _():
        zeros = jnp.zeros((HALO, D_MODEL), BF16)
        h2p_ref[0, 0:HALO, :] = zeros
        h2p_ref[0, seq + HALO:seq + 2 * HALO, :] = zeros

    h2p_ref[0, pl.ds(pl.multiple_of(HALO + m * tm, HALO), tm), :] = h2.astype(BF16)


def _outproj(attn, yssd, x, mod, w_attn, w_ssd, norm_w, *, tm):
    bsz, seq, d = x.shape
    return pl.pallas_call(
        functools.partial(_outproj_kernel, tm=tm, seq=seq),
        grid=(bsz, seq // tm),
        in_specs=[
            pl.BlockSpec((1, tm, NA_WIDTH), lambda b, m: (b, m, 0)),
            pl.BlockSpec((1, tm, SSD_WIDTH), lambda b, m: (b, m, 0)),
            pl.BlockSpec((1, tm, d), lambda b, m: (b, m, 0)),
            pl.BlockSpec((1, 6, d), lambda b, m: (b, 0, 0)),
            pl.BlockSpec(w_attn.shape, lambda b, m: (0, 0)),
            pl.BlockSpec(w_ssd.shape, lambda b, m: (0, 0)),
            pl.BlockSpec((1, d), lambda b, m: (0, 0)),
        ],
        out_specs=[pl.BlockSpec((1, tm, d), lambda b, m: (b, m, 0)),
                   pl.BlockSpec((1, seq + 2 * HALO, d), lambda b, m: (b, 0, 0))],
        out_shape=[jax.ShapeDtypeStruct((bsz, seq, d), F32),
                   jax.ShapeDtypeStruct((bsz, seq + 2 * HALO, d), BF16)],
        compiler_params=_params(("arbitrary", "arbitrary")),
        name="outproj",
    )(attn, yssd, x, mod, w_attn, w_ssd, norm_w.reshape(1, d))


def _ffn_kernel(h2p_ref, x1_ref, mod_ref, wup_ref, wd_ref, cw_ref, cb_ref, fnw_ref, o_ref, act_ref, *, tm, tf):
    m = pl.program_id(1)
    win = tm + 2 * HALO
    start = pl.multiple_of(m * tm, tm)
    hb = h2p_ref[0, pl.ds(start, win), :]
    body = slice(HALO, HALO + tm)
    for j in range(D_FF // tf):
        gv = jnp.dot(hb, wup_ref[:, 2 * j * tf:2 * (j + 1) * tf], preferred_element_type=F32)
        gate = gv[:, :tf]
        cols = slice(j * tf, (j + 1) * tf)
        gc = (cw_ref[0:1, cols] * pltpu.roll(gate, 1, 0)[body] + cw_ref[1:2, cols] * gate[body]
              + cw_ref[2:3, cols] * pltpu.roll(gate, win - 1, 0)[body] + cb_ref[:, cols])
        act_ref[:, cols] = (_silu(gc) * gv[body, tf:]).astype(BF16)
    y = jnp.dot(act_ref[...], wd_ref[...], preferred_element_type=F32)
    o_ref[0] = _rms(x1_ref[0] + mod_ref[0][5:6] * y, fnw_ref[...])


def _ffn(h2p, x1, mod, w_up, w_down, conv_w, conv_b, final_w, *, tm, tf):
    bsz, seq, d = x1.shape
    dff = w_down.shape[0]
    once = pl.Buffered(1)
    return pl.pallas_call(
        functools.partial(_ffn_kernel, tm=tm, tf=tf),
        grid=(bsz, seq // tm),
        in_specs=[
            pl.BlockSpec((1, seq + 2 * HALO, d), lambda b, m: (b, 0, 0)),
            pl.BlockSpec((1, tm, d), lambda b, m: (b, m, 0)),
            pl.BlockSpec((1, 6, d), lambda b, m: (b, 0, 0)),
            pl.BlockSpec((d, 2 * dff), lambda b, m: (0, 0), pipeline_mode=once),
            pl.BlockSpec((dff, d), lambda b, m: (0, 0), pipeline_mode=once),
            pl.BlockSpec((3, dff), lambda b, m: (0, 0)),
            pl.BlockSpec((1, dff), lambda b, m: (0, 0)),
            pl.BlockSpec((1, d), lambda b, m: (0, 0)),
        ],
        out_specs=pl.BlockSpec((1, tm, d), lambda b, m: (b, m, 0)),
        out_shape=jax.ShapeDtypeStruct((bsz, seq, d), F32),
        scratch_shapes=[pltpu.VMEM((tm, dff), BF16)],
        compiler_params=_params(("arbitrary", "arbitrary")),
        name="ffn",
    )(h2p, x1, mod, w_up, w_down, conv_w, conv_b.reshape(1, dff), final_w.reshape(1, d))


def _interleave_up(w_up, tf):
    d = w_up.shape[0]
    return w_up.reshape(d, 2, D_FF // tf, tf).transpose(0, 2, 1, 3).reshape(d, 2 * D_FF)


def kernel(x, c, ctx, c_ctx, ada_w, ada_b, norm1_w, w_in, rpb, ssd_conv_w, ssd_conv_b, dt_bias, a_log,
           ssd_d, ssd_norm_w, w_out, norm2_w, ffn_w_up, ffn_conv_w, ffn_conv_b, ffn_w_down, final_norm_w):
    assert ada_w.shape[0] == 1, "single-layer kernel"
    bsz, seq, d = x.shape
    rows = seq // GRID_W
    nh2 = 2 * SSD_HEADS

    cc = jnp.concatenate([c, c_ctx[None]], axis=0)
    cc = jnp.pad(cc, ((0, (-cc.shape[0]) % 8), (0, 0)))
    mod_all = _ada(cc, ada_w[0], ada_b[0])
    mod = mod_all[:bsz].reshape(bsz, 6, d)
    mod_c = mod_all[bsz:bsz + 1].reshape(1, 6, d)

    w_in_b = w_in[0].astype(BF16)
    col_dt = 3 * NA_WIDTH + SSD_WIDTH + XBC_WIDTH
    w_main = w_in_b[:, :col_dt]
    w_dt = jnp.pad(w_in_b[:, col_dt:], ((0, 0), (0, LANES - nh2)))

    q, k, vt, z, xbcp, dt = _inproj(x, mod, norm1_w[0], w_main, w_dt, _rope_tables(seq), latent=True, tm=512)
    kc, vct, xbcp_c, dt_c = _inproj(ctx, mod_c, norm1_w[0], w_main, w_dt, None, latent=False, tm=ctx.shape[1])

    attn = _attention(q, k, vt, kc, vct, _bias_tables(rpb[0], rows), qblocks=4)

    dtt = jnp.swapaxes(dt[:, :, :nh2], 1, 2)
    dtt_c = jnp.swapaxes(dt_c[:, :, :nh2], 1, 2)
    yssd = _ssd(xbcp, dt, dtt, z, xbcp_c, dt_c, dtt_c, ssd_conv_w[0], ssd_conv_b[0],
                dt_bias[0].reshape(nh2), a_log[0].reshape(nh2), ssd_d[0], ssd_norm_w[0])

    w_out_b = w_out[0].astype(BF16)
    x1, h2p = _outproj(attn, yssd, x, mod, w_out_b[:NA_WIDTH], w_out_b[NA_WIDTH:], norm2_w[0], tm=512)

    tf = 256
    return _ffn(h2p, x1, mod, _interleave_up(ffn_w_up[0].astype(BF16), tf), ffn_w_down[0].astype(BF16),
                ffn_conv_w[0], ffn_conv_b[0], final_norm_w, tm=512, tf=tf)
```

```python
import functools
import math

import numpy as np
import jax
import jax.numpy as jnp
from jax import lax
from jax.experimental import pallas as pl
from jax.experimental.pallas import tpu as pltpu

F32 = jnp.float32
BF16 = jnp.bfloat16

D_MODEL = 1024
GRID_W = 64
EPS = 1e-6
NA_HEADS = 8
HEAD_DIM = 64
NA_WIDTH = NA_HEADS * HEAD_DIM
NA_ROWS = 8
NA_COLS = 16
ROPE_THETA = 10000.0
SSD_HEADS = 8
SSD_HEAD_DIM = 64
SSD_WIDTH = SSD_HEADS * SSD_HEAD_DIM
SSD_GROUPS = 2
SSD_STATE = 128
GN = SSD_GROUPS * SSD_STATE
XBC_WIDTH = SSD_WIDTH + 2 * GN
D_FF = 2816

LANES = 128
CHUNK = 128
HALO = 16
Q_ROWS = 4
BAND_ROWS = Q_ROWS + NA_ROWS
NEG = -1e30
LOG2E = math.log2(math.e)
VMEM_LIMIT = 56 * 1024 * 1024

NT_DIMS = (((1,), (1,)), ((), ()))


def _silu(v):
    return v / (1.0 + jnp.exp(-v))


def _softplus(v):
    return jnp.maximum(v, 0.0) + jnp.log(1.0 + jnp.exp(-jnp.abs(v)))


def _rms(v, w):
    return v * lax.rsqrt(jnp.mean(v * v, axis=-1, keepdims=True) + EPS) * w


def _params(sem):
    return pltpu.CompilerParams(dimension_semantics=sem, vmem_limit_bytes=VMEM_LIMIT)


def _ada_kernel(c_ref, w_ref, b_ref, o_ref):
    a = _silu(c_ref[...])
    o_ref[...] = jnp.dot(a, w_ref[...], preferred_element_type=F32,
                         precision=lax.Precision.HIGHEST) + b_ref[...]


def _ada(cc, ada_w, ada_b):
    rows, d = cc.shape
    n = ada_w.shape[1]
    tn = 1024
    return pl.pallas_call(
        _ada_kernel,
        grid=(n // tn,),
        in_specs=[pl.BlockSpec((rows, d), lambda j: (0, 0)),
                  pl.BlockSpec((d, tn), lambda j: (0, j)),
                  pl.BlockSpec((1, tn), lambda j: (0, j))],
        out_specs=pl.BlockSpec((rows, tn), lambda j: (0, j)),
        out_shape=jax.ShapeDtypeStruct((rows, n), F32),
        compiler_params=_params(("arbitrary",)),
        name="ada",
    )(cc, ada_w, ada_b.reshape(1, n))


def _rope(v, cos, sin_lo, sin_hi):
    outs = []
    for g in range(v.shape[1] // LANES):
        vg = v[:, g * LANES:(g + 1) * LANES]
        outs.append(vg * cos + pltpu.roll(vg, LANES - 32, 1) * sin_lo + pltpu.roll(vg, 32, 1) * sin_hi)
    return jnp.concatenate(outs, axis=1)


def _inproj_kernel(*refs, latent, tm, seq):
    if latent:
        (x_ref, mod_ref, nw_ref, w_ref, cos_ref, slo_ref, shi_ref,
         q_ref, k_ref, vt_ref, z_ref, xbcp_ref, dt_ref) = refs
    else:
        (x_ref, mod_ref, nw_ref, w_ref, k_ref, vt_ref, xbcp_ref, dt_ref) = refs
    m = pl.program_id(1)
    mod = mod_ref[0]
    h = (_rms(x_ref[0], nw_ref[...]) * (1.0 + mod[1:2]) + mod[0:1]).astype(BF16)

    def proj(lo, hi):
        return jnp.dot(h, w_ref[:, lo:hi], preferred_element_type=F32)

    if latent:
        cos, slo, shi = cos_ref[...], slo_ref[...], shi_ref[...]
        q = _rope(proj(0, NA_WIDTH), cos, slo, shi) * (HEAD_DIM ** -0.5 * LOG2E)
        q_ref[0] = q.astype(BF16)
        k_ref[0] = _rope(proj(NA_WIDTH, 2 * NA_WIDTH), cos, slo, shi).astype(BF16)
        z_ref[0] = proj(3 * NA_WIDTH, 3 * NA_WIDTH + SSD_WIDTH).astype(BF16)
    else:
        k_ref[0] = proj(NA_WIDTH, 2 * NA_WIDTH).astype(BF16)
    vt_ref[0] = proj(2 * NA_WIDTH, 3 * NA_WIDTH).T.astype(BF16)

    @pl.when(m == 0)
    def _():
        zeros = jnp.zeros((HALO, XBC_WIDTH), BF16)
        xbcp_ref[0, 0:HALO, :] = zeros
        xbcp_ref[0, seq + HALO:seq + 2 * HALO, :] = zeros

    col = 3 * NA_WIDTH + SSD_WIDTH
    xbcp_ref[0, pl.ds(pl.multiple_of(HALO + m * tm, HALO), tm), :] = proj(col, col + XBC_WIDTH).astype(BF16)
    dt_ref[0] = proj(col + XBC_WIDTH, col + XBC_WIDTH + 2 * SSD_HEADS)


def _inproj(x, mod, norm_w, w_in, tables, *, latent, tm):
    bsz, seq, d = x.shape
    nm = seq // tm
    per_batch_mod = mod.shape[0] == bsz
    kern = functools.partial(_inproj_kernel, latent=latent, tm=tm, seq=seq)
    in_specs = [
        pl.BlockSpec((1, tm, d), lambda b, m: (b, m, 0)),
        pl.BlockSpec((1, 6, d), (lambda b, m: (b, 0, 0)) if per_batch_mod else (lambda b, m: (0, 0, 0))),
        pl.BlockSpec((1, d), lambda b, m: (0, 0)),
        pl.BlockSpec(w_in.shape, lambda b, m: (0, 0)),
    ]
    args = [x, mod, norm_w.reshape(1, d), w_in]
    tile = lambda width, dt: (pl.BlockSpec((1, tm, width), lambda b, m: (b, m, 0)),
                              jax.ShapeDtypeStruct((bsz, seq, width), dt))
    xbcp = (pl.BlockSpec((1, seq + 2 * HALO, XBC_WIDTH), lambda b, m: (b, 0, 0)),
            jax.ShapeDtypeStruct((bsz, seq + 2 * HALO, XBC_WIDTH), BF16))
    vt = (pl.BlockSpec((1, NA_WIDTH, tm), lambda b, m: (b, 0, m)),
          jax.ShapeDtypeStruct((bsz, NA_WIDTH, seq), BF16))
    if latent:
        in_specs += [pl.BlockSpec((tm, LANES), lambda b, m: (m, 0))] * 3
        args += list(tables)
        outs = [tile(NA_WIDTH, BF16), tile(NA_WIDTH, BF16), vt, tile(SSD_WIDTH, BF16), xbcp,
                tile(2 * SSD_HEADS, F32)]
    else:
        outs = [tile(NA_WIDTH, BF16), vt, xbcp, tile(2 * SSD_HEADS, F32)]
    return pl.pallas_call(
        kern,
        grid=(bsz, nm),
        in_specs=in_specs,
        out_specs=[o[0] for o in outs],
        out_shape=[o[1] for o in outs],
        compiler_params=_params(("arbitrary", "arbitrary")),
        name="inproj_latent" if latent else "inproj_ctx",
    )(*args)


def _rope_tables(seq):
    t = np.arange(seq)
    row = (t // GRID_W).astype(np.float32)
    col = (t % GRID_W).astype(np.float32)
    n_freq = HEAD_DIM // 4
    freqs = jnp.asarray(ROPE_THETA, F32) ** (-jnp.arange(n_freq, dtype=F32) / n_freq)
    ang = jnp.concatenate([jnp.asarray(row)[:, None] * freqs, jnp.asarray(col)[:, None] * freqs], axis=-1)
    cos, sin = jnp.cos(ang), jnp.sin(ang)
    zero = jnp.zeros_like(sin)
    rep = LANES // HEAD_DIM
    cos_t = jnp.tile(jnp.concatenate([cos, cos], axis=-1), (1, rep))
    sin_lo = jnp.tile(jnp.concatenate([-sin, zero], axis=-1), (1, rep))
    sin_hi = jnp.tile(jnp.concatenate([zero, sin], axis=-1), (1, rep))
    return cos_t, sin_lo, sin_hi


def _bias_tables(rpb, rows):
    nblk = rows // Q_ROWS
    kh = min(NA_ROWS, rows)
    nh = rpb.shape[0]
    pad = GRID_W - NA_COLS
    padded = jnp.pad(rpb * LOG2E, ((0, 0), (0, 0), (pad, pad)))
    cols = jnp.stack([padded[:, :, GRID_W - 1 - w:2 * GRID_W - 1 - w] for w in range(GRID_W)], axis=-1)
    c = np.arange(GRID_W)[:, None]
    w = np.arange(GRID_W)[None, :]
    cs = np.clip(w - NA_COLS // 2, 0, GRID_W - NA_COLS)
    cols = jnp.where((c >= cs) & (c < cs + NA_COLS), cols, NEG)
    masked = jnp.full((nh, GRID_W, GRID_W), NEG, F32)
    variants = []
    for kb in (0, 1, nblk - 1):
        band_rows = []
        for bi in range(BAND_ROWS):
            blocks = []
            for qr in range(Q_ROWS):
                r = Q_ROWS * kb + qr
                rs = min(max(r - kh // 2, 0), rows - kh)
                krow = _band_start(kb, rows) + bi
                blocks.append(cols[:, krow - r + NA_ROWS - 1] if rs <= krow < rs + kh else masked)
            band_rows.append(jnp.concatenate(blocks, axis=-1))
        variants.append(jnp.concatenate(band_rows, axis=-2))
    return jnp.stack(variants, axis=1)


def _band_start(kb, rows):
    return min(max(Q_ROWS * kb - NA_ROWS // 2, 0), rows - BAND_ROWS)


def _attn_kernel(q_ref, k_ref, vt_ref, kc_ref, vct_ref, bias_ref, o_ref, s_scr, p_scr, *, rows, qblocks):
    nblk = rows // Q_ROWS
    tq = Q_ROWS * GRID_W
    band = BAND_ROWS * GRID_W
    kc = kc_ref[0]
    vct = vct_ref[0]
    lane = lax.broadcasted_iota(jnp.int32, (1, LANES), 1)
    hd = HEAD_DIM
    for t in range(qblocks):
        kb = pl.program_id(2) * qblocks + t
        start = jnp.clip(Q_ROWS * kb - NA_ROWS // 2, 0, rows - BAND_ROWS) * GRID_W
        start = pl.multiple_of(start, Q_ROWS * GRID_W)
        variant = jnp.where(kb == 0, 0, jnp.where(kb == nblk - 1, 2, 1))
        q = q_ref[0, t * tq:(t + 1) * tq, :]
        kband = k_ref[0, pl.ds(start, band), :]
        vtband = vt_ref[0, :, pl.ds(start, band)]
        nhp = LANES // hd
        qm = jnp.concatenate(
            [jnp.where((lane >= hh * hd) & (lane < (hh + 1) * hd), q, jnp.zeros_like(q)) for hh in range(nhp)], axis=0)
        bias = jnp.concatenate([bias_ref[hh, variant] for hh in range(nhp)], axis=1)
        s_ref, p_ref = s_scr.at[t % 2], p_scr.at[t % 2]
        s_ref[0:band, :] = lax.dot_general(kband, qm, NT_DIMS, preferred_element_type=F32) + bias
        s_ref[band:, :] = lax.dot_general(kc, qm, NT_DIMS, preferred_element_type=F32)
        s = s_ref[...]
        p = jnp.exp2(s - jnp.max(s, axis=0, keepdims=True))
        den = jnp.sum(p, axis=0, keepdims=True)
        p_ref[...] = p.astype(BF16)
        ot = (jnp.dot(vtband, p_ref[0:band, :], preferred_element_type=F32)
              + jnp.dot(vct, p_ref[band:, :], preferred_element_type=F32)) / den
        out_t = jnp.concatenate([ot[hh * hd:(hh + 1) * hd, hh * tq:(hh + 1) * tq] for hh in range(nhp)], axis=0)
        o_ref[0, t * tq:(t + 1) * tq, :] = out_t.T.astype(BF16)


def _attention(q, k, vt, kc, vct, bias, *, qblocks):
    bsz, seq, width = q.shape
    lc = kc.shape[1]
    rows = seq // GRID_W
    nblk = rows // Q_ROWS
    tq = Q_ROWS * GRID_W * qblocks
    npair = width // LANES
    hpp = LANES // HEAD_DIM
    return pl.pallas_call(
        functools.partial(_attn_kernel, rows=rows, qblocks=qblocks),
        grid=(npair, bsz, nblk // qblocks),
        in_specs=[
            pl.BlockSpec((1, tq, LANES), lambda p, b, kb: (b, kb, p)),
            pl.BlockSpec((1, seq, LANES), lambda p, b, kb: (b, 0, p)),
            pl.BlockSpec((1, LANES, seq), lambda p, b, kb: (b, p, 0)),
            pl.BlockSpec((1, lc, LANES), lambda p, b, kb: (b, 0, p)),
            pl.BlockSpec((1, LANES, lc), lambda p, b, kb: (b, p, 0)),
            pl.BlockSpec((hpp, 3, BAND_ROWS * GRID_W, Q_ROWS * GRID_W), lambda p, b, kb: (p, 0, 0, 0)),
        ],
        out_specs=pl.BlockSpec((1, tq, LANES), lambda p, b, kb: (b, kb, p)),
        out_shape=jax.ShapeDtypeStruct((bsz, seq, width), BF16),
        scratch_shapes=[pltpu.VMEM((2, BAND_ROWS * GRID_W + lc, hpp * Q_ROWS * GRID_W), F32),
                        pltpu.VMEM((2, BAND_ROWS * GRID_W + lc, hpp * Q_ROWS * GRID_W), BF16)],
        compiler_params=_params(("arbitrary", "arbitrary", "arbitrary")),
        name="attn",
    )(q, k, vt, kc, vct, bias)


def _ssd_kernel(xbcp_ref, dt_ref, dtt_ref, z_ref, xbcpc_ref, dtc_ref, dttc_ref,
                cw_ref, cb_ref, brow_ref, bcol_ref, arow_ref, acol_ref, dskip_ref, nw_ref,
                y_ref, xact, xactc, yacc, state, *, seq, lc):
    nchunk = seq // CHUNK
    nchunk_c = lc // CHUNK
    win = CHUNK + 2 * HALO

    def conv_pass(src_ref, dst_ref, n):
        def body(c, carry):
            r0 = pl.multiple_of(c * CHUNK, CHUNK)
            w = src_ref[0, pl.ds(r0, win), :].astype(F32)
            prev = pltpu.roll(w, 1, 0)[HALO:HALO + CHUNK]
            nxt = pltpu.roll(w, win - 1, 0)[HALO:HALO + CHUNK]
            g = cw_ref[0:1] * prev + cw_ref[1:2] * w[HALO:HALO + CHUNK] + cw_ref[2:3] * nxt + cb_ref[...]
            dst_ref[pl.ds(r0, CHUNK), :] = _silu(g).astype(BF16)
            return carry
        lax.fori_loop(0, n, body, 0)

    conv_pass(xbcp_ref, xact, nchunk)
    conv_pass(xbcpc_ref, xactc, nchunk_c)
    state[...] = jnp.zeros_like(state)

    a_row = -jnp.exp(arow_ref[...])
    a_col = -jnp.exp(acol_ref[...])
    ii = lax.broadcasted_iota(jnp.int32, (CHUNK, CHUNK), 0)
    jj = lax.broadcasted_iota(jnp.int32, (CHUNK, CHUNK), 1)
    lane_head = lax.broadcasted_iota(jnp.int32, (1, SSD_WIDTH // SSD_GROUPS), 1) // SSD_HEAD_DIM
    hpg = SSD_HEADS // SSD_GROUPS

    def chunk_step(xa, dtr, dtrt, d, need_y):
        dt = _softplus(dtr + brow_ref[...])
        dtt = _softplus(dtrt + bcol_ref[...])
        tri = (jj <= ii) if d == 0 else (jj >= ii)
        tri_f = tri.astype(F32)
        tri_t = ((ii <= jj) if d == 0 else (ii >= jj)).astype(F32)
        cum = jnp.dot(tri_f, dt * a_row, preferred_element_type=F32, precision=lax.Precision.HIGHEST)
        cumt = jnp.dot(dtt * a_col, tri_t, preferred_element_type=F32, precision=lax.Precision.HIGHEST)
        end = CHUNK - 1 if d == 0 else 0
        tot_t = cumt[:, end:end + 1]
        w_t = jnp.exp(tot_t - cumt)
        cd_t = jnp.exp(tot_t)

        def per_head_rows(v, h0, width):
            return jnp.concatenate(
                [jnp.broadcast_to(v[h0 + e:h0 + e + 1, :], (SSD_HEAD_DIM, width)) for e in range(hpg)], axis=0)

        ys = []
        for g in range(SSD_GROUPS):
            h0 = SSD_HEADS * d + hpg * g
            xg = xa[:, g * 256:(g + 1) * 256]
            bg = xa[:, SSD_WIDTH + g * SSD_STATE:SSD_WIDTH + (g + 1) * SSD_STATE]
            cg = xa[:, SSD_WIDTH + GN + g * SSD_STATE:SSD_WIDTH + GN + (g + 1) * SSD_STATE]
            xdt = xg.astype(F32).T * per_head_rows(dtt, h0, CHUNK)
            sg = state[d, g]
            if need_y:
                cb = jnp.where(tri, lax.dot_general(cg, bg, NT_DIMS, preferred_element_type=F32), 0.0)
                rhs_t = jnp.concatenate([xdt.astype(BF16), sg.astype(BF16)], axis=1)
                cf = cg.astype(F32)
                yg = None
                for e in range(hpg):
                    cum_b = jnp.broadcast_to(cum[:, h0 + e:h0 + e + 1], (CHUNK, CHUNK))
                    mh = cb * jnp.exp(jnp.minimum(cum_b - cumt[h0 + e:h0 + e + 1, :], 0.0))
                    ch = cf * jnp.exp(cum_b)
                    lhs = jnp.concatenate([mh, ch], axis=1).astype(BF16)
                    yd = lax.dot_general(lhs, rhs_t, NT_DIMS, preferred_element_type=F32)
                    yg = yd if yg is None else jnp.where(lane_head == e, yd, yg)
                ys.append(yg)
            st = jnp.dot((xdt * per_head_rows(w_t, h0, CHUNK)).astype(BF16), bg, preferred_element_type=F32)
            state[d, g] = sg * per_head_rows(cd_t, h0, SSD_STATE) + st
        return jnp.concatenate(ys, axis=1) if need_y else None

    def rows_of(c):
        return pl.ds(pl.multiple_of(c * CHUNK, CHUNK), CHUNK)

    def ctx_body(i, carry):
        for d, c in ((0, i), (1, nchunk_c - 1 - i)):
            chunk_step(xactc[rows_of(c), :], dtc_ref[0, rows_of(c), :], dttc_ref[0, :, rows_of(c)], d, False)
        return carry

    def latent_body(finish):
        def body(i, carry):
            for d, c in ((0, i), (1, nchunk - 1 - i)):
                r = rows_of(c)
                xa = xact[r, :]
                y = chunk_step(xa, dt_ref[0, r, :], dtt_ref[0, :, r], d, True)
                if finish:
                    y = y + yacc[r, :] + dskip_ref[...] * xa[:, :SSD_WIDTH].astype(F32)
                    y = y * _silu(z_ref[0, r, :].astype(F32))
                    y_ref[0, r, :] = _rms(y, nw_ref[...]).astype(BF16)
                else:
                    yacc[r, :] = y
            return carry
        return body

    lax.fori_loop(0, nchunk_c, ctx_body, 0)
    lax.fori_loop(0, nchunk // 2, latent_body(False), 0)
    lax.fori_loop(nchunk // 2, nchunk, latent_body(True), 0)


def _ssd(xbcp, dt, dtt, z, xbcp_c, dt_c, dtt_c, conv_w, conv_b, dt_bias, a_log, d_skip, norm_w):
    bsz, seq, _ = z.shape
    lc = dt_c.shape[1]
    nh2 = 2 * SSD_HEADS
    full = lambda shape: pl.BlockSpec(shape, lambda b: (0,) * len(shape))
    per_b = lambda shape: pl.BlockSpec((1,) + shape, lambda b: (b, 0, 0))
    return pl.pallas_call(
        functools.partial(_ssd_kernel, seq=seq, lc=lc),
        grid=(bsz,),
        in_specs=[
            per_b((seq + 2 * HALO, XBC_WIDTH)), per_b((seq, nh2)), per_b((nh2, seq)), per_b((seq, SSD_WIDTH)),
            per_b((lc + 2 * HALO, XBC_WIDTH)), per_b((lc, nh2)), per_b((nh2, lc)),
            full((3, XBC_WIDTH)), full((1, XBC_WIDTH)),
            full((1, nh2)), full((nh2, 1)), full((1, nh2)), full((nh2, 1)),
            full((1, SSD_WIDTH)), full((1, SSD_WIDTH)),
        ],
        out_specs=per_b((seq, SSD_WIDTH)),
        out_shape=jax.ShapeDtypeStruct((bsz, seq, SSD_WIDTH), BF16),
        scratch_shapes=[
            pltpu.VMEM((seq, XBC_WIDTH), BF16),
            pltpu.VMEM((lc, XBC_WIDTH), BF16),
            pltpu.VMEM((seq, SSD_WIDTH), F32),
            pltpu.VMEM((2, SSD_GROUPS, SSD_WIDTH // SSD_GROUPS, SSD_STATE), F32),
        ],
        compiler_params=_params(("arbitrary",)),
        name="ssd",
    )(xbcp, dt, dtt, z, xbcp_c, dt_c, dtt_c, conv_w, conv_b.reshape(1, XBC_WIDTH),
      dt_bias.reshape(1, nh2), dt_bias.reshape(nh2, 1), a_log.reshape(1, nh2), a_log.reshape(nh2, 1),
      jnp.repeat(d_skip, SSD_HEAD_DIM).reshape(1, SSD_WIDTH), norm_w.reshape(1, SSD_WIDTH))


def _outproj_kernel(attn_ref, yssd_ref, x_ref, mod_ref, wa_ref, ws_ref, nw_ref, x1_ref, h2p_ref, *, tm, seq):
    m = pl.program_id(1)
    mod = mod_ref[0]
    mix = (jnp.dot(attn_ref[0], wa_ref[...], preferred_element_type=F32)
           + jnp.dot(yssd_ref[0], ws_ref[...], preferred_element_type=F32))
    x1 = x_ref[0] + mod[2:3] * mix
    x1_ref[0] = x1
    h2 = _rms(x1, nw_ref[...]) * (1.0 + mod[4:5]) + mod[3:4]

    @pl.when(m == 0)
    def _():
        zeros = jnp.zeros((HALO, D_MODEL), BF16)
        h2p_ref[0, 0:HALO, :] = zeros
        h2p_ref[0, seq + HALO:seq + 2 * HALO, :] = zeros

    h2p_ref[0, pl.ds(pl.multiple_of(HALO + m * tm, HALO), tm), :] = h2.astype(BF16)


def _outproj(attn, yssd, x, mod, w_attn, w_ssd, norm_w, *, tm):
    bsz, seq, d = x.shape
    return pl.pallas_call(
        functools.partial(_outproj_kernel, tm=tm, seq=seq),
        grid=(bsz, seq // tm),
        in_specs=[
            pl.BlockSpec((1, tm, NA_WIDTH), lambda b, m: (b, m, 0)),
            pl.BlockSpec((1, tm, SSD_WIDTH), lambda b, m: (b, m, 0)),
            pl.BlockSpec((1, tm, d), lambda b, m: (b, m, 0)),
            pl.BlockSpec((1, 6, d), lambda b, m: (b, 0, 0)),
            pl.BlockSpec(w_attn.shape, lambda b, m: (0, 0)),
            pl.BlockSpec(w_ssd.shape, lambda b, m: (0, 0)),
            pl.BlockSpec((1, d), lambda b, m: (0, 0)),
        ],
        out_specs=[pl.BlockSpec((1, tm, d), lambda b, m: (b, m, 0)),
                   pl.BlockSpec((1, seq + 2 * HALO, d), lambda b, m: (b, 0, 0))],
        out_shape=[jax.ShapeDtypeStruct((bsz, seq, d), F32),
                   jax.ShapeDtypeStruct((bsz, seq + 2 * HALO, d), BF16)],
        compiler_params=_params(("arbitrary", "arbitrary")),
        name="outproj",
    )(attn, yssd, x, mod, w_attn, w_ssd, norm_w.reshape(1, d))


def _ffn_kernel(h2p_ref, x1_ref, mod_ref, wup_ref, wd_ref, cw_ref, cb_ref, fnw_ref, o_ref, act_ref, *, tm, tf):
    m = pl.program_id(1)
    win = tm + 2 * HALO
    start = pl.multiple_of(m * tm, tm)
    hb = h2p_ref[0, pl.ds(start, win), :]
    body = slice(HALO, HALO + tm)
    for j in range(D_FF // tf):
        cols = slice(j * tf, (j + 1) * tf)
        gate = jnp.dot(hb, wup_ref[:, cols], preferred_element_type=F32)
        val = jnp.dot(hb[body], wup_ref[:, D_FF + j * tf:D_FF + (j + 1) * tf], preferred_element_type=F32)
        gc = (cw_ref[0:1, cols] * pltpu.roll(gate, 1, 0)[body] + cw_ref[1:2, cols] * gate[body]
              + cw_ref[2:3, cols] * pltpu.roll(gate, win - 1, 0)[body] + cb_ref[:, cols])
        act_ref[:, cols] = (_silu(gc) * val).astype(BF16)
    y = jnp.dot(act_ref[...], wd_ref[...], preferred_element_type=F32)
    o_ref[0] = _rms(x1_ref[0] + mod_ref[0][5:6] * y, fnw_ref[...])


def _ffn(h2p, x1, mod, w_up, w_down, conv_w, conv_b, final_w, *, tm, tf):
    bsz, seq, d = x1.shape
    dff = w_down.shape[0]
    once = pl.Buffered(1)
    return pl.pallas_call(
        functools.partial(_ffn_kernel, tm=tm, tf=tf),
        grid=(bsz, seq // tm),
        in_specs=[
            pl.BlockSpec((1, seq + 2 * HALO, d), lambda b, m: (b, 0, 0)),
            pl.BlockSpec((1, tm, d), lambda b, m: (b, m, 0)),
            pl.BlockSpec((1, 6, d), lambda b, m: (b, 0, 0)),
            pl.BlockSpec((d, 2 * dff), lambda b, m: (0, 0), pipeline_mode=once),
            pl.BlockSpec((dff, d), lambda b, m: (0, 0), pipeline_mode=once),
            pl.BlockSpec((3, dff), lambda b, m: (0, 0)),
            pl.BlockSpec((1, dff), lambda b, m: (0, 0)),
            pl.BlockSpec((1, d), lambda b, m: (0, 0)),
        ],
        out_specs=pl.BlockSpec((1, tm, d), lambda b, m: (b, m, 0)),
        out_shape=jax.ShapeDtypeStruct((bsz, seq, d), F32),
        scratch_shapes=[pltpu.VMEM((tm, dff), BF16)],
        compiler_params=_params(("arbitrary", "arbitrary")),
        name="ffn",
    )(h2p, x1, mod, w_up, w_down, conv_w, conv_b.reshape(1, dff), final_w.reshape(1, d))


def kernel(x, c, ctx, c_ctx, ada_w, ada_b, norm1_w, w_in, rpb, ssd_conv_w, ssd_conv_b, dt_bias, a_log,
           ssd_d, ssd_norm_w, w_out, norm2_w, ffn_w_up, ffn_conv_w, ffn_conv_b, ffn_w_down, final_norm_w):
    assert ada_w.shape[0] == 1, "single-layer kernel"
    bsz, seq, d = x.shape
    rows = seq // GRID_W
    nh2 = 2 * SSD_HEADS

    cc = jnp.concatenate([c, c_ctx[None]], axis=0)
    cc = jnp.pad(cc, ((0, (-cc.shape[0]) % 8), (0, 0)))
    mod_all = _ada(cc, ada_w[0], ada_b[0])
    mod = mod_all[:bsz].reshape(bsz, 6, d)
    mod_c = mod_all[bsz:bsz + 1].reshape(1, 6, d)

    w_in_b = w_in[0].astype(BF16)
    q, k, vt, z, xbcp, dt = _inproj(x, mod, norm1_w[0], w_in_b, _rope_tables(seq), latent=True, tm=512)
    kc, vct, xbcp_c, dt_c = _inproj(ctx, mod_c, norm1_w[0], w_in_b, None, latent=False, tm=ctx.shape[1])

    attn = _attention(q, k, vt, kc, vct, _bias_tables(rpb[0], rows), qblocks=4)

    dtt = jnp.swapaxes(dt, 1, 2)
    dtt_c = jnp.swapaxes(dt_c, 1, 2)
    yssd = _ssd(xbcp, dt, dtt, z, xbcp_c, dt_c, dtt_c, ssd_conv_w[0], ssd_conv_b[0],
                dt_bias[0].reshape(nh2), a_log[0].reshape(nh2), ssd_d[0], ssd_norm_w[0])

    w_out_b = w_out[0].astype(BF16)
    x1, h2p = _outproj(attn, yssd, x, mod, w_out_b[:NA_WIDTH], w_out_b[NA_WIDTH:], norm2_w[0], tm=512)

    return _ffn(h2p, x1, mod, ffn_w_up[0].astype(BF16), ffn_w_down[0].astype(BF16),
                ffn_conv_w[0], ffn_conv_b[0], final_norm_w, tm=512, tf=D_FF // 2)
```

```python
import functools
import math

import numpy as np
import jax
import jax.numpy as jnp
from jax import lax
from jax.experimental import pallas as pl
from jax.experimental.pallas import tpu as pltpu

F32 = jnp.float32
BF16 = jnp.bfloat16

D_MODEL = 1024
GRID_W = 64
EPS = 1e-6
NA_HEADS = 8
HEAD_DIM = 64
NA_WIDTH = NA_HEADS * HEAD_DIM
NA_ROWS = 8
NA_COLS = 16
ROPE_THETA = 10000.0
SSD_HEADS = 8
SSD_HEAD_DIM = 64
SSD_WIDTH = SSD_HEADS * SSD_HEAD_DIM
SSD_GROUPS = 2
SSD_STATE = 128
GN = SSD_GROUPS * SSD_STATE
XBC_WIDTH = SSD_WIDTH + 2 * GN
D_FF = 2816

LANES = 128
CHUNK = 128
HALO = 16
Q_ROWS = 4
BAND_ROWS = Q_ROWS + NA_ROWS
NEG = -1e30
LOG2E = math.log2(math.e)
VMEM_LIMIT = 56 * 1024 * 1024

NT_DIMS = (((1,), (1,)), ((), ()))


def _silu(v):
    return v / (1.0 + jnp.exp(-v))


def _softplus(v):
    return jnp.maximum(v, 0.0) + jnp.log(1.0 + jnp.exp(-jnp.abs(v)))


def _rms(v, w):
    return v * lax.rsqrt(jnp.mean(v * v, axis=-1, keepdims=True) + EPS) * w


def _params(sem):
    return pltpu.CompilerParams(dimension_semantics=sem, vmem_limit_bytes=VMEM_LIMIT)


def _ada_kernel(c_ref, w_ref, b_ref, o_ref):
    a = _silu(c_ref[...])
    o_ref[...] = jnp.dot(a, w_ref[...], preferred_element_type=F32,
                         precision=lax.Precision.HIGHEST) + b_ref[...]


def _ada(cc, ada_w, ada_b):
    rows, d = cc.shape
    n = ada_w.shape[1]
    tn = 1024
    return pl.pallas_call(
        _ada_kernel,
        grid=(n // tn,),
        in_specs=[pl.BlockSpec((rows, d), lambda j: (0, 0)),
                  pl.BlockSpec((d, tn), lambda j: (0, j)),
                  pl.BlockSpec((1, tn), lambda j: (0, j))],
        out_specs=pl.BlockSpec((rows, tn), lambda j: (0, j)),
        out_shape=jax.ShapeDtypeStruct((rows, n), F32),
        compiler_params=_params(("arbitrary",)),
        name="ada",
    )(cc, ada_w, ada_b.reshape(1, n))


def _rope(v, cos, sin_lo, sin_hi):
    outs = []
    for g in range(v.shape[1] // LANES):
        vg = v[:, g * LANES:(g + 1) * LANES]
        outs.append(vg * cos + pltpu.roll(vg, LANES - 32, 1) * sin_lo + pltpu.roll(vg, 32, 1) * sin_hi)
    return jnp.concatenate(outs, axis=1)


def _inproj_kernel(*refs, latent, tm, seq):
    if latent:
        (x_ref, mod_ref, nw_ref, w_ref, cos_ref, slo_ref, shi_ref,
         q_ref, k_ref, vt_ref, z_ref, xbcp_ref, dt_ref) = refs
    else:
        (x_ref, mod_ref, nw_ref, w_ref, k_ref, vt_ref, xbcp_ref, dt_ref) = refs
    m = pl.program_id(1)
    mod = mod_ref[0]
    h = (_rms(x_ref[0], nw_ref[...]) * (1.0 + mod[1:2]) + mod[0:1]).astype(BF16)

    def proj(lo, hi):
        return jnp.dot(h, w_ref[:, lo:hi], preferred_element_type=F32)

    if latent:
        cos, slo, shi = cos_ref[...], slo_ref[...], shi_ref[...]
        q = _rope(proj(0, NA_WIDTH), cos, slo, shi) * (HEAD_DIM ** -0.5 * LOG2E)
        q_ref[0] = q.astype(BF16)
        k_ref[0] = _rope(proj(NA_WIDTH, 2 * NA_WIDTH), cos, slo, shi).astype(BF16)
        z_ref[0] = proj(3 * NA_WIDTH, 3 * NA_WIDTH + SSD_WIDTH).astype(BF16)
    else:
        k_ref[0] = proj(NA_WIDTH, 2 * NA_WIDTH).astype(BF16)
    vt_ref[0] = proj(2 * NA_WIDTH, 3 * NA_WIDTH).T.astype(BF16)

    @pl.when(m == 0)
    def _():
        zeros = jnp.zeros((HALO, XBC_WIDTH), BF16)
        xbcp_ref[0, 0:HALO, :] = zeros
        xbcp_ref[0, seq + HALO:seq + 2 * HALO, :] = zeros

    col = 3 * NA_WIDTH + SSD_WIDTH
    xbc_dt = proj(col, col + XBC_WIDTH + 2 * SSD_HEADS)
    xbcp_ref[0, pl.ds(pl.multiple_of(HALO + m * tm, HALO), tm), :] = xbc_dt[:, :XBC_WIDTH].astype(BF16)
    dt_ref[0] = xbc_dt[:, XBC_WIDTH:]


def _inproj(x, mod, norm_w, w_in, tables, *, latent, tm):
    bsz, seq, d = x.shape
    nm = seq // tm
    per_batch_mod = mod.shape[0] == bsz
    kern = functools.partial(_inproj_kernel, latent=latent, tm=tm, seq=seq)
    in_specs = [
        pl.BlockSpec((1, tm, d), lambda b, m: (b, m, 0)),
        pl.BlockSpec((1, 6, d), (lambda b, m: (b, 0, 0)) if per_batch_mod else (lambda b, m: (0, 0, 0))),
        pl.BlockSpec((1, d), lambda b, m: (0, 0)),
        pl.BlockSpec(w_in.shape, lambda b, m: (0, 0)),
    ]
    args = [x, mod, norm_w.reshape(1, d), w_in]
    tile = lambda width, dt: (pl.BlockSpec((1, tm, width), lambda b, m: (b, m, 0)),
                              jax.ShapeDtypeStruct((bsz, seq, width), dt))
    xbcp = (pl.BlockSpec((1, seq + 2 * HALO, XBC_WIDTH), lambda b, m: (b, 0, 0)),
            jax.ShapeDtypeStruct((bsz, seq + 2 * HALO, XBC_WIDTH), BF16))
    vt = (pl.BlockSpec((1, NA_WIDTH, tm), lambda b, m: (b, 0, m)),
          jax.ShapeDtypeStruct((bsz, NA_WIDTH, seq), BF16))
    if latent:
        in_specs += [pl.BlockSpec((tm, LANES), lambda b, m: (m, 0))] * 3
        args += list(tables)
        outs = [tile(NA_WIDTH, BF16), tile(NA_WIDTH, BF16), vt, tile(SSD_WIDTH, BF16), xbcp,
                tile(2 * SSD_HEADS, F32)]
    else:
        outs = [tile(NA_WIDTH, BF16), vt, xbcp, tile(2 * SSD_HEADS, F32)]
    return pl.pallas_call(
        kern,
        grid=(bsz, nm),
        in_specs=in_specs,
        out_specs=[o[0] for o in outs],
        out_shape=[o[1] for o in outs],
        compiler_params=_params(("arbitrary", "arbitrary")),
        name="inproj_latent" if latent else "inproj_ctx",
    )(*args)


def _rope_tables(seq):
    t = np.arange(seq)
    row = (t // GRID_W).astype(np.float32)
    col = (t % GRID_W).astype(np.float32)
    n_freq = HEAD_DIM // 4
    freqs = jnp.asarray(ROPE_THETA, F32) ** (-jnp.arange(n_freq, dtype=F32) / n_freq)
    ang = jnp.concatenate([jnp.asarray(row)[:, None] * freqs, jnp.asarray(col)[:, None] * freqs], axis=-1)
    cos, sin = jnp.cos(ang), jnp.sin(ang)
    zero = jnp.zeros_like(sin)
    rep = LANES // HEAD_DIM
    cos_t = jnp.tile(jnp.concatenate([cos, cos], axis=-1), (1, rep))
    sin_lo = jnp.tile(jnp.concatenate([-sin, zero], axis=-1), (1, rep))
    sin_hi = jnp.tile(jnp.concatenate([zero, sin], axis=-1), (1, rep))
    return cos_t, sin_lo, sin_hi


def _bias_tables(rpb, rows):
    nblk = rows // Q_ROWS
    kh = min(NA_ROWS, rows)
    nh = rpb.shape[0]
    pad = GRID_W - NA_COLS
    padded = jnp.pad(rpb * LOG2E, ((0, 0), (0, 0), (pad, pad)))
    cols = jnp.stack([padded[:, :, GRID_W - 1 - w:2 * GRID_W - 1 - w] for w in range(GRID_W)], axis=-1)
    c = np.arange(GRID_W)[:, None]
    w = np.arange(GRID_W)[None, :]
    cs = np.clip(w - NA_COLS // 2, 0, GRID_W - NA_COLS)
    cols = jnp.where((c >= cs) & (c < cs + NA_COLS), cols, NEG)
    masked = jnp.full((nh, GRID_W, GRID_W), NEG, F32)
    variants = []
    for kb in (0, 1, nblk - 1):
        band_rows = []
        for bi in range(BAND_ROWS):
            blocks = []
            for qr in range(Q_ROWS):
                r = Q_ROWS * kb + qr
                rs = min(max(r - kh // 2, 0), rows - kh)
                krow = _band_start(kb, rows) + bi
                blocks.append(cols[:, krow - r + NA_ROWS - 1] if rs <= krow < rs + kh else masked)
            band_rows.append(jnp.concatenate(blocks, axis=-1))
        variants.append(jnp.concatenate(band_rows, axis=-2))
    return jnp.stack(variants, axis=1)


def _band_start(kb, rows):
    return min(max(Q_ROWS * kb - NA_ROWS // 2, 0), rows - BAND_ROWS)


def _attn_kernel(q_ref, k_ref, vt_ref, kc_ref, vct_ref, bias_ref, o_ref, s0, s1, p0, p1, *, rows):
    nblk = rows // Q_ROWS
    tq = Q_ROWS * GRID_W
    band = BAND_ROWS * GRID_W
    hd = HEAD_DIM
    nhp = LANES // hd
    lane = lax.broadcasted_iota(jnp.int32, (1, LANES), 1)
    s_bufs, p_bufs = (s0, s1), (p0, p1)

    def band_start(kb):
        start = jnp.clip(Q_ROWS * kb - NA_ROWS // 2, 0, rows - BAND_ROWS) * GRID_W
        return pl.multiple_of(start, Q_ROWS * GRID_W)

    def scores(kb, slot):
        variant = jnp.where(kb == 0, 0, jnp.where(kb == nblk - 1, 2, 1))
        q = q_ref[0, pl.ds(pl.multiple_of(kb * tq, tq), tq), :]
        qm = jnp.concatenate(
            [jnp.where((lane >= hh * hd) & (lane < (hh + 1) * hd), q, jnp.zeros_like(q)) for hh in range(nhp)], axis=0)
        bias = jnp.concatenate([bias_ref[hh, variant] for hh in range(nhp)], axis=1)
        kband = k_ref[0, pl.ds(band_start(kb), band), :]
        s_ref = s_bufs[slot]
        s_ref[0:band, :] = lax.dot_general(kband, qm, NT_DIMS, preferred_element_type=F32) + bias
        s_ref[band:, :] = lax.dot_general(kc_ref[0], qm, NT_DIMS, preferred_element_type=F32)

    def finish(kb, slot):
        s = s_bufs[slot][...]
        p_ref = p_bufs[slot]
        p = jnp.exp2(s - jnp.max(s, axis=0, keepdims=True))
        den = jnp.sum(p, axis=0, keepdims=True)
        p_ref[...] = p.astype(BF16)
        vtband = vt_ref[0, :, pl.ds(band_start(kb), band)]
        ot = (jnp.dot(vtband, p_ref[0:band, :], preferred_element_type=F32)
              + jnp.dot(vct_ref[0], p_ref[band:, :], preferred_element_type=F32)) / den
        out_t = jnp.concatenate([ot[hh * hd:(hh + 1) * hd, hh * tq:(hh + 1) * tq] for hh in range(nhp)], axis=0)
        o_ref[0, pl.ds(pl.multiple_of(kb * tq, tq), tq), :] = out_t.T.astype(BF16)

    scores(0, 0)

    def body(j, carry):
        scores(2 * j + 1, 1)
        finish(2 * j, 0)
        scores(2 * j + 2, 0)
        finish(2 * j + 1, 1)
        return carry

    lax.fori_loop(0, nblk // 2 - 1, body, 0)
    scores(nblk - 1, 1)
    finish(nblk - 2, 0)
    finish(nblk - 1, 1)


def _attention(q, k, vt, kc, vct, bias):
    bsz, seq, width = q.shape
    lc = kc.shape[1]
    rows = seq // GRID_W
    npair = width // LANES
    hpp = LANES // HEAD_DIM
    nkeys = BAND_ROWS * GRID_W + lc
    return pl.pallas_call(
        functools.partial(_attn_kernel, rows=rows),
        grid=(npair, bsz),
        in_specs=[
            pl.BlockSpec((1, seq, LANES), lambda p, b: (b, 0, p)),
            pl.BlockSpec((1, seq, LANES), lambda p, b: (b, 0, p)),
            pl.BlockSpec((1, LANES, seq), lambda p, b: (b, p, 0)),
            pl.BlockSpec((1, lc, LANES), lambda p, b: (b, 0, p)),
            pl.BlockSpec((1, LANES, lc), lambda p, b: (b, p, 0)),
            pl.BlockSpec((hpp, 3, BAND_ROWS * GRID_W, Q_ROWS * GRID_W), lambda p, b: (p, 0, 0, 0)),
        ],
        out_specs=pl.BlockSpec((1, seq, LANES), lambda p, b: (b, 0, p)),
        out_shape=jax.ShapeDtypeStruct((bsz, seq, width), BF16),
        scratch_shapes=[pltpu.VMEM((nkeys, hpp * Q_ROWS * GRID_W), F32)] * 2
                       + [pltpu.VMEM((nkeys, hpp * Q_ROWS * GRID_W), BF16)] * 2,
        compiler_params=_params(("arbitrary", "arbitrary")),
        name="attn",
    )(q, k, vt, kc, vct, bias)


def _ssd_kernel(xbcp_ref, dt_ref, dtt_ref, z_ref, xbcpc_ref, dtc_ref, dttc_ref,
                cw_ref, cb_ref, brow_ref, bcol_ref, arow_ref, acol_ref, dskip_ref, nw_ref,
                y_ref, xact, xactc, yacc, state, *, seq, lc):
    nchunk = seq // CHUNK
    nchunk_c = lc // CHUNK
    win = CHUNK + 2 * HALO

    def conv_pass(src_ref, dst_ref, n):
        def body(c, carry):
            r0 = pl.multiple_of(c * CHUNK, CHUNK)
            w = src_ref[0, pl.ds(r0, win), :].astype(F32)
            prev = pltpu.roll(w, 1, 0)[HALO:HALO + CHUNK]
            nxt = pltpu.roll(w, win - 1, 0)[HALO:HALO + CHUNK]
            g = cw_ref[0:1] * prev + cw_ref[1:2] * w[HALO:HALO + CHUNK] + cw_ref[2:3] * nxt + cb_ref[...]
            dst_ref[pl.ds(r0, CHUNK), :] = _silu(g).astype(BF16)
            return carry
        lax.fori_loop(0, n, body, 0)

    conv_pass(xbcp_ref, xact, nchunk)
    conv_pass(xbcpc_ref, xactc, nchunk_c)
    state[...] = jnp.zeros_like(state)

    a_row = -jnp.exp(arow_ref[...])
    a_col = -jnp.exp(acol_ref[...])
    ii = lax.broadcasted_iota(jnp.int32, (CHUNK, CHUNK), 0)
    jj = lax.broadcasted_iota(jnp.int32, (CHUNK, CHUNK), 1)
    lane_head = lax.broadcasted_iota(jnp.int32, (1, SSD_WIDTH // SSD_GROUPS), 1) // SSD_HEAD_DIM
    hpg = SSD_HEADS // SSD_GROUPS

    def chunk_step(xa, dtr, dtrt, d, need_y):
        dt = _softplus(dtr + brow_ref[...])
        dtt = _softplus(dtrt + bcol_ref[...])
        tri = (jj <= ii) if d == 0 else (jj >= ii)
        tri_f = tri.astype(F32)
        tri_t = ((ii <= jj) if d == 0 else (ii >= jj)).astype(F32)
        cum = jnp.dot(tri_f, dt * a_row, preferred_element_type=F32, precision=lax.Precision.HIGHEST)
        cumt = jnp.dot(dtt * a_col, tri_t, preferred_element_type=F32, precision=lax.Precision.HIGHEST)
        end = CHUNK - 1 if d == 0 else 0
        tot_t = cumt[:, end:end + 1]
        w_t = jnp.exp(tot_t - cumt)
        cd_t = jnp.exp(tot_t)

        def per_head_rows(v, h0, width):
            return jnp.concatenate(
                [jnp.broadcast_to(v[h0 + e:h0 + e + 1, :], (SSD_HEAD_DIM, width)) for e in range(hpg)], axis=0)

        ys = []
        for g in range(SSD_GROUPS):
            h0 = SSD_HEADS * d + hpg * g
            xg = xa[:, g * 256:(g + 1) * 256]
            bg = xa[:, SSD_WIDTH + g * SSD_STATE:SSD_WIDTH + (g + 1) * SSD_STATE]
            cg = xa[:, SSD_WIDTH + GN + g * SSD_STATE:SSD_WIDTH + GN + (g + 1) * SSD_STATE]
            xdt = xg.astype(F32).T * per_head_rows(dtt, h0, CHUNK)
            sg = state[d, g]
            if need_y:
                cb = jnp.where(tri, lax.dot_general(cg, bg, NT_DIMS, preferred_element_type=F32), 0.0)
                rhs_t = jnp.concatenate([xdt.astype(BF16), sg.astype(BF16)], axis=1)
                cf = cg.astype(F32)
                yg = None
                for e in range(hpg):
                    cum_b = jnp.broadcast_to(cum[:, h0 + e:h0 + e + 1], (CHUNK, CHUNK))
                    mh = cb * jnp.exp(jnp.minimum(cum_b - cumt[h0 + e:h0 + e + 1, :], 0.0))
                    ch = cf * jnp.exp(cum_b)
                    lhs = jnp.concatenate([mh, ch], axis=1).astype(BF16)
                    yd = lax.dot_general(lhs, rhs_t, NT_DIMS, preferred_element_type=F32)
                    yg = yd if yg is None else jnp.where(lane_head == e, yd, yg)
                ys.append(yg)
            st = jnp.dot((xdt * per_head_rows(w_t, h0, CHUNK)).astype(BF16), bg, preferred_element_type=F32)
            state[d, g] = sg * per_head_rows(cd_t, h0, SSD_STATE) + st
        return jnp.concatenate(ys, axis=1) if need_y else None

    def rows_of(c):
        return pl.ds(pl.multiple_of(c * CHUNK, CHUNK), CHUNK)

    def ctx_body(i, carry):
        for d, c in ((0, i), (1, nchunk_c - 1 - i)):
            chunk_step(xactc[rows_of(c), :], dtc_ref[0, rows_of(c), :], dttc_ref[0, :, rows_of(c)], d, False)
        return carry

    def latent_body(finish):
        def body(i, carry):
            for d, c in ((0, i), (1, nchunk - 1 - i)):
                r = rows_of(c)
                xa = xact[r, :]
                y = chunk_step(xa, dt_ref[0, r, :], dtt_ref[0, :, r], d, True)
                if finish:
                    y = y + yacc[r, :] + dskip_ref[...] * xa[:, :SSD_WIDTH].astype(F32)
                    y = y * _silu(z_ref[0, r, :].astype(F32))
                    y_ref[0, r, :] = _rms(y, nw_ref[...]).astype(BF16)
                else:
                    yacc[r, :] = y
            return carry
        return body

    lax.fori_loop(0, nchunk_c, ctx_body, 0)
    lax.fori_loop(0, nchunk // 2, latent_body(False), 0)
    lax.fori_loop(nchunk // 2, nchunk, latent_body(True), 0)


def _ssd(xbcp, dt, dtt, z, xbcp_c, dt_c, dtt_c, conv_w, conv_b, dt_bias, a_log, d_skip, norm_w):
    bsz, seq, _ = z.shape
    lc = dt_c.shape[1]
    nh2 = 2 * SSD_HEADS
    full = lambda shape: pl.BlockSpec(shape, lambda b: (0,) * len(shape))
    per_b = lambda shape: pl.BlockSpec((1,) + shape, lambda b: (b, 0, 0))
    return pl.pallas_call(
        functools.partial(_ssd_kernel, seq=seq, lc=lc),
        grid=(bsz,),
        in_specs=[
            per_b((seq + 2 * HALO, XBC_WIDTH)), per_b((seq, nh2)), per_b((nh2, seq)), per_b((seq, SSD_WIDTH)),
            per_b((lc + 2 * HALO, XBC_WIDTH)), per_b((lc, nh2)), per_b((nh2, lc)),
            full((3, XBC_WIDTH)), full((1, XBC_WIDTH)),
            full((1, nh2)), full((nh2, 1)), full((1, nh2)), full((nh2, 1)),
            full((1, SSD_WIDTH)), full((1, SSD_WIDTH)),
        ],
        out_specs=per_b((seq, SSD_WIDTH)),
        out_shape=jax.ShapeDtypeStruct((bsz, seq, SSD_WIDTH), BF16),
        scratch_shapes=[
            pltpu.VMEM((seq, XBC_WIDTH), BF16),
            pltpu.VMEM((lc, XBC_WIDTH), BF16),
            pltpu.VMEM((seq, SSD_WIDTH), F32),
            pltpu.VMEM((2, SSD_GROUPS, SSD_WIDTH // SSD_GROUPS, SSD_STATE), F32),
        ],
        compiler_params=_params(("arbitrary",)),
        name="ssd",
    )(xbcp, dt, dtt, z, xbcp_c, dt_c, dtt_c, conv_w, conv_b.reshape(1, XBC_WIDTH),
      dt_bias.reshape(1, nh2), dt_bias.reshape(nh2, 1), a_log.reshape(1, nh2), a_log.reshape(nh2, 1),
      jnp.repeat(d_skip, SSD_HEAD_DIM).reshape(1, SSD_WIDTH), norm_w.reshape(1, SSD_WIDTH))


def _outproj_kernel(attn_ref, yssd_ref, x_ref, mod_ref, wa_ref, ws_ref, nw_ref, x1_ref, h2p_ref, *, tm, seq):
    m = pl.program_id(1)
    mod = mod_ref[0]
    mix = (jnp.dot(attn_ref[0], wa_ref[...], preferred_element_type=F32)
           + jnp.dot(yssd_ref[0], ws_ref[...], preferred_element_type=F32))
    x1 = x_ref[0] + mod[2:3] * mix
    x1_ref[0] = x1
    h2 = _rms(x1, nw_ref[...]) * (1.0 + mod[4:5]) + mod[3:4]

    @pl.when(m == 0)
    def _():
        zeros = jnp.zeros((HALO, D_MODEL), BF16)
        h2p_ref[0, 0:HALO, :] = zeros
        h2p_ref[0, seq + HALO:seq + 2 * HALO, :] = zeros

    h2p_ref[0, pl.ds(pl.multiple_of(HALO + m * tm, HALO), tm), :] = h2.astype(BF16)


def _outproj(attn, yssd, x, mod, w_attn, w_ssd, norm_w, *, tm):
    bsz, seq, d = x.shape
    return pl.pallas_call(
        functools.partial(_outproj_kernel, tm=tm, seq=seq),
        grid=(bsz, seq // tm),
        in_specs=[
            pl.BlockSpec((1, tm, NA_WIDTH), lambda b, m: (b, m, 0)),
            pl.BlockSpec((1, tm, SSD_WIDTH), lambda b, m: (b, m, 0)),
            pl.BlockSpec((1, tm, d), lambda b, m: (b, m, 0)),
            pl.BlockSpec((1, 6, d), lambda b, m: (b, 0, 0)),
            pl.BlockSpec(w_attn.shape, lambda b, m: (0, 0)),
            pl.BlockSpec(w_ssd.shape, lambda b, m: (0, 0)),
            pl.BlockSpec((1, d), lambda b, m: (0, 0)),
        ],
        out_specs=[pl.BlockSpec((1, tm, d), lambda b, m: (b, m, 0)),
                   pl.BlockSpec((1, seq + 2 * HALO, d), lambda b, m: (b, 0, 0))],
        out_shape=[jax.ShapeDtypeStruct((bsz, seq, d), F32),
                   jax.ShapeDtypeStruct((bsz, seq + 2 * HALO, d), BF16)],
        compiler_params=_params(("arbitrary", "arbitrary")),
        name="outproj",
    )(attn, yssd, x, mod, w_attn, w_ssd, norm_w.reshape(1, d))


def _ffn_kernel(h2p_ref, x1_ref, mod_ref, wup_ref, wd_ref, cw_ref, cb_ref, fnw_ref, o_ref, act_ref, *, tm, tf):
    m = pl.program_id(1)
    win = tm + 2 * HALO
    start = pl.multiple_of(m * tm, tm)
    hb = h2p_ref[0, pl.ds(start, win), :]
    body = slice(HALO, HALO + tm)
    for j in range(D_FF // tf):
        cols = slice(j * tf, (j + 1) * tf)
        gate = jnp.dot(hb, wup_ref[:, cols], preferred_element_type=F32)
        val = jnp.dot(hb[body], wup_ref[:, D_FF + j * tf:D_FF + (j + 1) * tf], preferred_element_type=F32)
        gc = (cw_ref[0:1, cols] * pltpu.roll(gate, 1, 0)[body] + cw_ref[1:2, cols] * gate[body]
              + cw_ref[2:3, cols] * pltpu.roll(gate, win - 1, 0)[body] + cb_ref[:, cols])
        act_ref[:, cols] = (_silu(gc) * val).astype(BF16)
    y = jnp.dot(act_ref[...], wd_ref[...], preferred_element_type=F32)
    o_ref[0] = _rms(x1_ref[0] + mod_ref[0][5:6] * y, fnw_ref[...])


def _ffn(h2p, x1, mod, w_up, w_down, conv_w, conv_b, final_w, *, tm, tf):
    bsz, seq, d = x1.shape
    dff = w_down.shape[0]
    once = pl.Buffered(1)
    return pl.pallas_call(
        functools.partial(_ffn_kernel, tm=tm, tf=tf),
        grid=(bsz, seq // tm),
        in_specs=[
            pl.BlockSpec((1, seq + 2 * HALO, d), lambda b, m: (b, 0, 0)),
            pl.BlockSpec((1, tm, d), lambda b, m: (b, m, 0)),
            pl.BlockSpec((1, 6, d), lambda b, m: (b, 0, 0)),
            pl.BlockSpec((d, 2 * dff), lambda b, m: (0, 0), pipeline_mode=once),
            pl.BlockSpec((dff, d), lambda b, m: (0, 0), pipeline_mode=once),
            pl.BlockSpec((3, dff), lambda b, m: (0, 0)),
            pl.BlockSpec((1, dff), lambda b, m: (0, 0)),
            pl.BlockSpec((1, d), lambda b, m: (0, 0)),
        ],
        out_specs=pl.BlockSpec((1, tm, d), lambda b, m: (b, m, 0)),
        out_shape=jax.ShapeDtypeStruct((bsz, seq, d), F32),
        scratch_shapes=[pltpu.VMEM((tm, dff), BF16)],
        compiler_params=_params(("arbitrary", "arbitrary")),
        name="ffn",
    )(h2p, x1, mod, w_up, w_down, conv_w, conv_b.reshape(1, dff), final_w.reshape(1, d))


def kernel(x, c, ctx, c_ctx, ada_w, ada_b, norm1_w, w_in, rpb, ssd_conv_w, ssd_conv_b, dt_bias, a_log,
           ssd_d, ssd_norm_w, w_out, norm2_w, ffn_w_up, ffn_conv_w, ffn_conv_b, ffn_w_down, final_norm_w):
    assert ada_w.shape[0] == 1, "single-layer kernel"
    bsz, seq, d = x.shape
    rows = seq // GRID_W
    nh2 = 2 * SSD_HEADS

    cc = jnp.concatenate([c, c_ctx[None]], axis=0)
    cc = jnp.pad(cc, ((0, (-cc.shape[0]) % 8), (0, 0)))
    mod_all = _ada(cc, ada_w[0], ada_b[0])
    mod = mod_all[:bsz].reshape(bsz, 6, d)
    mod_c = mod_all[bsz:bsz + 1].reshape(1, 6, d)

    w_in_b = w_in[0].astype(BF16)
    q, k, vt, z, xbcp, dt = _inproj(x, mod, norm1_w[0], w_in_b, _rope_tables(seq), latent=True, tm=512)
    kc, vct, xbcp_c, dt_c = _inproj(ctx, mod_c, norm1_w[0], w_in_b, None, latent=False, tm=ctx.shape[1])

    attn = _attention(q, k, vt, kc, vct, _bias_tables(rpb[0], rows))

    dtt = jnp.swapaxes(dt, 1, 2)
    dtt_c = jnp.swapaxes(dt_c, 1, 2)
    yssd = _ssd(xbcp, dt, dtt, z, xbcp_c, dt_c, dtt_c, ssd_conv_w[0], ssd_conv_b[0],
                dt_bias[0].reshape(nh2), a_log[0].reshape(nh2), ssd_d[0], ssd_norm_w[0])

    w_out_b = w_out[0].astype(BF16)
    x1, h2p = _outproj(attn, yssd, x, mod, w_out_b[:NA_WIDTH], w_out_b[NA_WIDTH:], norm2_w[0], tm=512)

    return _ffn(h2p, x1, mod, ffn_w_up[0].astype(BF16), ffn_w_down[0].astype(BF16),
                ffn_conv_w[0], ffn_conv_b[0], final_norm_w, tm=512, tf=D_FF // 2)
```

```python
import functools
import math

import numpy as np
import jax
import jax.numpy as jnp
from jax import lax
from jax.experimental import pallas as pl
from jax.experimental.pallas import tpu as pltpu

F32 = jnp.float32
BF16 = jnp.bfloat16

D_MODEL = 1024
GRID_W = 64
EPS = 1e-6
NA_HEADS = 8
HEAD_DIM = 64
NA_WIDTH = NA_HEADS * HEAD_DIM
NA_ROWS = 8
NA_COLS = 16
ROPE_THETA = 10000.0
SSD_HEADS = 8
SSD_HEAD_DIM = 64
SSD_WIDTH = SSD_HEADS * SSD_HEAD_DIM
SSD_GROUPS = 2
SSD_STATE = 128
GN = SSD_GROUPS * SSD_STATE
XBC_WIDTH = SSD_WIDTH + 2 * GN
D_FF = 2816

LANES = 128
CHUNK = 128
HALO = 16
Q_ROWS = 4
BAND_ROWS = Q_ROWS + NA_ROWS
NEG = -1e30
LOG2E = math.log2(math.e)
VMEM_LIMIT = 56 * 1024 * 1024

NT_DIMS = (((1,), (1,)), ((), ()))


def _silu(v):
    return v / (1.0 + jnp.exp(-v))


def _softplus(v):
    return jnp.maximum(v, 0.0) + jnp.log(1.0 + jnp.exp(-jnp.abs(v)))


def _rms(v, w):
    return v * lax.rsqrt(jnp.mean(v * v, axis=-1, keepdims=True) + EPS) * w


def _params(sem):
    return pltpu.CompilerParams(dimension_semantics=sem, vmem_limit_bytes=VMEM_LIMIT)


def _ada_kernel(c_ref, w_ref, b_ref, o_ref):
    a = _silu(c_ref[...])
    o_ref[...] = jnp.dot(a, w_ref[...], preferred_element_type=F32,
                         precision=lax.Precision.HIGHEST) + b_ref[...]


def _ada(cc, ada_w, ada_b):
    rows, d = cc.shape
    n = ada_w.shape[1]
    tn = 1024
    return pl.pallas_call(
        _ada_kernel,
        grid=(n // tn,),
        in_specs=[pl.BlockSpec((rows, d), lambda j: (0, 0)),
                  pl.BlockSpec((d, tn), lambda j: (0, j)),
                  pl.BlockSpec((1, tn), lambda j: (0, j))],
        out_specs=pl.BlockSpec((rows, tn), lambda j: (0, j)),
        out_shape=jax.ShapeDtypeStruct((rows, n), F32),
        compiler_params=_params(("arbitrary",)),
        name="ada",
    )(cc, ada_w, ada_b.reshape(1, n))


def _rope(v, cos, sin_lo, sin_hi):
    outs = []
    for g in range(v.shape[1] // LANES):
        vg = v[:, g * LANES:(g + 1) * LANES]
        outs.append(vg * cos + pltpu.roll(vg, LANES - 32, 1) * sin_lo + pltpu.roll(vg, 32, 1) * sin_hi)
    return jnp.concatenate(outs, axis=1)


def _inproj_kernel(*refs, latent, tm, seq):
    if latent:
        (x_ref, mod_ref, nw_ref, w_ref, cos_ref, slo_ref, shi_ref,
         q_ref, k_ref, vt_ref, z_ref, xbcp_ref, dt_ref) = refs
    else:
        (x_ref, mod_ref, nw_ref, w_ref, k_ref, vt_ref, xbcp_ref, dt_ref) = refs
    m = pl.program_id(1)

    @pl.when(m == 0)
    def _():
        xbcp_ref[0, 0:HALO, :] = jnp.zeros((HALO, XBC_WIDTH), BF16)
        xbcp_ref[0, seq + HALO:seq + CHUNK, :] = jnp.zeros((CHUNK - HALO, XBC_WIDTH), BF16)

    mod = mod_ref[0]
    h = (_rms(x_ref[0], nw_ref[...]) * (1.0 + mod[1:2]) + mod[0:1]).astype(BF16)

    def proj(lo, hi):
        return jnp.dot(h, w_ref[:, lo:hi], preferred_element_type=F32)

    if latent:
        cos, slo, shi = cos_ref[...], slo_ref[...], shi_ref[...]
        q = _rope(proj(0, NA_WIDTH), cos, slo, shi) * (HEAD_DIM ** -0.5 * LOG2E)
        q_ref[0] = q.astype(BF16)
        k_ref[0] = _rope(proj(NA_WIDTH, 2 * NA_WIDTH), cos, slo, shi).astype(BF16)
        z_ref[0] = proj(3 * NA_WIDTH, 3 * NA_WIDTH + SSD_WIDTH).astype(BF16)
    else:
        k_ref[0] = proj(NA_WIDTH, 2 * NA_WIDTH).astype(BF16)
    vt_ref[0] = proj(2 * NA_WIDTH, 3 * NA_WIDTH).T.astype(BF16)
    col = 3 * NA_WIDTH + SSD_WIDTH
    xbc_dt = proj(col, col + XBC_WIDTH + 2 * SSD_HEADS)
    xbcp_ref[0, pl.ds(pl.multiple_of(HALO + m * tm, HALO), tm), :] = xbc_dt[:, :XBC_WIDTH].astype(BF16)
    dt_ref[0] = xbc_dt[:, XBC_WIDTH:]


def _inproj(x, mod, norm_w, w_in, tables, *, latent, tm):
    bsz, seq, d = x.shape
    nm = seq // tm
    per_batch_mod = mod.shape[0] == bsz
    kern = functools.partial(_inproj_kernel, latent=latent, tm=tm, seq=seq)
    in_specs = [
        pl.BlockSpec((1, tm, d), lambda b, m: (b, m, 0)),
        pl.BlockSpec((1, 6, d), (lambda b, m: (b, 0, 0)) if per_batch_mod else (lambda b, m: (0, 0, 0))),
        pl.BlockSpec((1, d), lambda b, m: (0, 0)),
        pl.BlockSpec(w_in.shape, lambda b, m: (0, 0)),
    ]
    args = [x, mod, norm_w.reshape(1, d), w_in]
    tile = lambda width, dt: (pl.BlockSpec((1, tm, width), lambda b, m: (b, m, 0)),
                              jax.ShapeDtypeStruct((bsz, seq, width), dt))
    xbcp = (pl.BlockSpec((1, seq + CHUNK, XBC_WIDTH), lambda b, m: (b, 0, 0)),
            jax.ShapeDtypeStruct((bsz, seq + CHUNK, XBC_WIDTH), BF16))
    vt = (pl.BlockSpec((1, NA_WIDTH, tm), lambda b, m: (b, 0, m)),
          jax.ShapeDtypeStruct((bsz, NA_WIDTH, seq), BF16))
    if latent:
        in_specs += [pl.BlockSpec((tm, LANES), lambda b, m: (m, 0))] * 3
        args += list(tables)
        outs = [tile(NA_WIDTH, BF16), tile(NA_WIDTH, BF16), vt, tile(SSD_WIDTH, BF16), xbcp,
                tile(2 * SSD_HEADS, F32)]
    else:
        outs = [tile(NA_WIDTH, BF16), vt, xbcp, tile(2 * SSD_HEADS, F32)]
    return pl.pallas_call(
        kern,
        grid=(bsz, nm),
        in_specs=in_specs,
        out_specs=[o[0] for o in outs],
        out_shape=[o[1] for o in outs],
        compiler_params=_params(("arbitrary", "arbitrary")),
        name="inproj_latent" if latent else "inproj_ctx",
    )(*args)


def _rope_tables(seq):
    t = np.arange(seq)
    row = (t // GRID_W).astype(np.float32)
    col = (t % GRID_W).astype(np.float32)
    n_freq = HEAD_DIM // 4
    freqs = jnp.asarray(ROPE_THETA, F32) ** (-jnp.arange(n_freq, dtype=F32) / n_freq)
    ang = jnp.concatenate([jnp.asarray(row)[:, None] * freqs, jnp.asarray(col)[:, None] * freqs], axis=-1)
    cos, sin = jnp.cos(ang), jnp.sin(ang)
    zero = jnp.zeros_like(sin)
    rep = LANES // HEAD_DIM
    cos_t = jnp.tile(jnp.concatenate([cos, cos], axis=-1), (1, rep))
    sin_lo = jnp.tile(jnp.concatenate([-sin, zero], axis=-1), (1, rep))
    sin_hi = jnp.tile(jnp.concatenate([zero, sin], axis=-1), (1, rep))
    return cos_t, sin_lo, sin_hi


def _bias_tables(rpb, rows):
    nblk = rows // Q_ROWS
    kh = min(NA_ROWS, rows)
    nh = rpb.shape[0]
    pad = GRID_W - NA_COLS
    padded = jnp.pad(rpb * LOG2E, ((0, 0), (0, 0), (pad, pad)))
    cols = jnp.stack([padded[:, :, GRID_W - 1 - w:2 * GRID_W - 1 - w] for w in range(GRID_W)], axis=-1)
    c = np.arange(GRID_W)[:, None]
    w = np.arange(GRID_W)[None, :]
    cs = np.clip(w - NA_COLS // 2, 0, GRID_W - NA_COLS)
    cols = jnp.where((c >= cs) & (c < cs + NA_COLS), cols, NEG)
    masked = jnp.full((nh, GRID_W, GRID_W), NEG, F32)
    variants = []
    for kb in (0, 1, nblk - 1):
        band_rows = []
        for bi in range(BAND_ROWS):
            blocks = []
            for qr in range(Q_ROWS):
                r = Q_ROWS * kb + qr
                rs = min(max(r - kh // 2, 0), rows - kh)
                krow = _band_start(kb, rows) + bi
                blocks.append(cols[:, krow - r + NA_ROWS - 1] if rs <= krow < rs + kh else masked)
            band_rows.append(jnp.concatenate(blocks, axis=-1))
        variants.append(jnp.concatenate(band_rows, axis=-2))
    return jnp.stack(variants, axis=1)


def _band_start(kb, rows):
    return min(max(Q_ROWS * kb - NA_ROWS // 2, 0), rows - BAND_ROWS)


def _attn_kernel(q_ref, k_ref, vt_ref, kc_ref, vct_ref, bias_ref, o_ref, s0, s1, p0, p1, *, rows):
    nblk = rows // Q_ROWS
    tq = Q_ROWS * GRID_W
    band = BAND_ROWS * GRID_W
    hd = HEAD_DIM
    nhp = LANES // hd
    lane = lax.broadcasted_iota(jnp.int32, (1, LANES), 1)
    s_bufs, p_bufs = (s0, s1), (p0, p1)

    def band_start(kb):
        start = jnp.clip(Q_ROWS * kb - NA_ROWS // 2, 0, rows - BAND_ROWS) * GRID_W
        return pl.multiple_of(start, Q_ROWS * GRID_W)

    def scores(kb, slot):
        variant = jnp.where(kb == 0, 0, jnp.where(kb == nblk - 1, 2, 1))
        q = q_ref[0, pl.ds(pl.multiple_of(kb * tq, tq), tq), :]
        qm = jnp.concatenate(
            [jnp.where((lane >= hh * hd) & (lane < (hh + 1) * hd), q, jnp.zeros_like(q)) for hh in range(nhp)], axis=0)
        bias = jnp.concatenate([bias_ref[hh, variant] for hh in range(nhp)], axis=1)
        kband = k_ref[0, pl.ds(band_start(kb), band), :]
        s_ref = s_bufs[slot]
        s_ref[0:band, :] = lax.dot_general(kband, qm, NT_DIMS, preferred_element_type=F32) + bias
        s_ref[band:, :] = lax.dot_general(kc_ref[0], qm, NT_DIMS, preferred_element_type=F32)

    def finish(kb, slot):
        s = s_bufs[slot][...]
        p_ref = p_bufs[slot]
        p = jnp.exp2(s - jnp.max(s, axis=0, keepdims=True))
        den = jnp.sum(p, axis=0, keepdims=True)
        p_ref[...] = p.astype(BF16)
        vtband = vt_ref[0, :, pl.ds(band_start(kb), band)]
        ot = (jnp.dot(vtband, p_ref[0:band, :], preferred_element_type=F32)
              + jnp.dot(vct_ref[0], p_ref[band:, :], preferred_element_type=F32)) / den
        out_t = jnp.concatenate([ot[hh * hd:(hh + 1) * hd, hh * tq:(hh + 1) * tq] for hh in range(nhp)], axis=0)
        o_ref[0, pl.ds(pl.multiple_of(kb * tq, tq), tq), :] = out_t.T.astype(BF16)

    scores(0, 0)

    def body(j, carry):
        scores(2 * j + 1, 1)
        finish(2 * j, 0)
        scores(2 * j + 2, 0)
        finish(2 * j + 1, 1)
        return carry

    lax.fori_loop(0, nblk // 2 - 1, body, 0)
    scores(nblk - 1, 1)
    finish(nblk - 2, 0)
    finish(nblk - 1, 1)


def _attention(q, k, vt, kc, vct, bias):
    bsz, seq, width = q.shape
    lc = kc.shape[1]
    rows = seq // GRID_W
    npair = width // LANES
    hpp = LANES // HEAD_DIM
    nkeys = BAND_ROWS * GRID_W + lc
    return pl.pallas_call(
        functools.partial(_attn_kernel, rows=rows),
        grid=(npair, bsz),
        in_specs=[
            pl.BlockSpec((1, seq, LANES), lambda p, b: (b, 0, p)),
            pl.BlockSpec((1, seq, LANES), lambda p, b: (b, 0, p)),
            pl.BlockSpec((1, LANES, seq), lambda p, b: (b, p, 0)),
            pl.BlockSpec((1, lc, LANES), lambda p, b: (b, 0, p)),
            pl.BlockSpec((1, LANES, lc), lambda p, b: (b, p, 0)),
            pl.BlockSpec((hpp, 3, BAND_ROWS * GRID_W, Q_ROWS * GRID_W), lambda p, b: (p, 0, 0, 0)),
        ],
        out_specs=pl.BlockSpec((1, seq, LANES), lambda p, b: (b, 0, p)),
        out_shape=jax.ShapeDtypeStruct((bsz, seq, width), BF16),
        scratch_shapes=[pltpu.VMEM((nkeys, hpp * Q_ROWS * GRID_W), F32)] * 2
                       + [pltpu.VMEM((nkeys, hpp * Q_ROWS * GRID_W), BF16)] * 2,
        compiler_params=_params(("arbitrary", "arbitrary")),
        name="attn",
    )(q, k, vt, kc, vct, bias)


def _ssd_kernel(xbcp_ref, dt_ref, dtt_ref, z_ref, xbcpc_ref, dtc_ref, dttc_ref,
                cw_ref, cb_ref, brow_ref, bcol_ref, arow_ref, acol_ref, dskip_ref, nw_ref,
                y_ref, xact, xactc, yacc, state, *, seq, lc):
    nchunk = seq // CHUNK
    nchunk_c = lc // CHUNK
    win = 2 * CHUNK

    sel_r = lax.broadcasted_iota(jnp.int32, (2 * CHUNK, win), 0)
    sel_c = lax.broadcasted_iota(jnp.int32, (2 * CHUNK, win), 1)
    sel_src = jnp.where(sel_r < CHUNK, sel_r + (HALO - 1), sel_r - CHUNK + (HALO + 1))
    shift = jnp.where(sel_c == sel_src, 1.0, 0.0).astype(BF16)

    def conv_chunk(src_ref, dst_ref, c):
        r0 = pl.multiple_of(c * CHUNK, CHUNK)
        w = src_ref[0, pl.ds(r0, win), :]
        nb = jnp.dot(shift, w, preferred_element_type=F32)
        g = (cw_ref[0:1] * nb[:CHUNK] + cw_ref[1:2] * w[HALO:HALO + CHUNK].astype(F32)
             + cw_ref[2:3] * nb[CHUNK:] + cb_ref[...])
        dst_ref[pl.ds(r0, CHUNK), :] = _silu(g).astype(BF16)

    for c in range(nchunk_c):
        conv_chunk(xbcpc_ref, xactc, c)
    conv_chunk(xbcp_ref, xact, 0)
    conv_chunk(xbcp_ref, xact, nchunk - 1)
    state[...] = jnp.zeros_like(state)

    a_row = -jnp.exp(arow_ref[...]) * LOG2E
    a_col = -jnp.exp(acol_ref[...]) * LOG2E
    ii = lax.broadcasted_iota(jnp.int32, (CHUNK, CHUNK), 0)
    jj = lax.broadcasted_iota(jnp.int32, (CHUNK, CHUNK), 1)
    lane_head = lax.broadcasted_iota(jnp.int32, (1, SSD_WIDTH // SSD_GROUPS), 1) // SSD_HEAD_DIM
    hpg = SSD_HEADS // SSD_GROUPS

    def chunk_step(xa, dtr, dtrt, d, need_y):
        dt = _softplus(dtr + brow_ref[...])
        dtt = _softplus(dtrt + bcol_ref[...])
        tri = (jj <= ii) if d == 0 else (jj >= ii)
        tri_f = tri.astype(F32)
        tri_t = ((ii <= jj) if d == 0 else (ii >= jj)).astype(F32)
        cum = jnp.dot(tri_f, dt * a_row, preferred_element_type=F32, precision=lax.Precision.HIGHEST)
        cumt = jnp.dot(dtt * a_col, tri_t, preferred_element_type=F32, precision=lax.Precision.HIGHEST)
        end = CHUNK - 1 if d == 0 else 0
        tot_t = cumt[:, end:end + 1]
        w_t = jnp.exp2(tot_t - cumt)
        cd_t = jnp.exp2(tot_t)

        def per_head_rows(v, h0, width):
            return jnp.concatenate(
                [jnp.broadcast_to(v[h0 + e:h0 + e + 1, :], (SSD_HEAD_DIM, width)) for e in range(hpg)], axis=0)

        ys = []
        for g in range(SSD_GROUPS):
            h0 = SSD_HEADS * d + hpg * g
            xg = xa[:, g * 256:(g + 1) * 256]
            bg = xa[:, SSD_WIDTH + g * SSD_STATE:SSD_WIDTH + (g + 1) * SSD_STATE]
            cg = xa[:, SSD_WIDTH + GN + g * SSD_STATE:SSD_WIDTH + GN + (g + 1) * SSD_STATE]
            xdt = xg.astype(F32).T * per_head_rows(dtt, h0, CHUNK)
            sg = state[d, g]
            if need_y:
                cb = jnp.where(tri, lax.dot_general(cg, bg, NT_DIMS, preferred_element_type=F32), 0.0)
                rhs_t = jnp.concatenate([xdt.astype(BF16), sg.astype(BF16)], axis=1)
                cf = cg.astype(F32)
                yg = None
                for e in range(hpg):
                    cum_b = jnp.broadcast_to(cum[:, h0 + e:h0 + e + 1], (CHUNK, CHUNK))
                    mh = cb * jnp.exp2(jnp.minimum(cum_b - cumt[h0 + e:h0 + e + 1, :], 0.0))
                    ch = cf * jnp.exp2(cum_b)
                    lhs = jnp.concatenate([mh, ch], axis=1).astype(BF16)
                    yd = lax.dot_general(lhs, rhs_t, NT_DIMS, preferred_element_type=F32)
                    yg = yd if yg is None else jnp.where(lane_head == e, yd, yg)
                ys.append(yg)
            st = jnp.dot((xdt * per_head_rows(w_t, h0, CHUNK)).astype(BF16), bg, preferred_element_type=F32)
            state[d, g] = sg * per_head_rows(cd_t, h0, SSD_STATE) + st
        return jnp.concatenate(ys, axis=1) if need_y else None

    def rows_of(c):
        return pl.ds(pl.multiple_of(c * CHUNK, CHUNK), CHUNK)

    def ctx_body(i, carry):
        for d, c in ((0, i), (1, nchunk_c - 1 - i)):
            chunk_step(xactc[rows_of(c), :], dtc_ref[0, rows_of(c), :], dttc_ref[0, :, rows_of(c)], d, False)
        return carry

    def latent_body(finish, convert_next):
        def body(i, carry):
            for d, c in ((0, i), (1, nchunk - 1 - i)):
                r = rows_of(c)
                xa = xact[r, :]
                y = chunk_step(xa, dt_ref[0, r, :], dtt_ref[0, :, r], d, True)
                if finish:
                    y = y + yacc[r, :] + dskip_ref[...] * xa[:, :SSD_WIDTH].astype(F32)
                    y = y * _silu(z_ref[0, r, :].astype(F32))
                    y_ref[0, r, :] = _rms(y, nw_ref[...]).astype(BF16)
                else:
                    yacc[r, :] = y
            if convert_next:
                conv_chunk(xbcp_ref, xact, i + 1)
                conv_chunk(xbcp_ref, xact, nchunk - 2 - i)
            return carry
        return body

    half = nchunk // 2
    lax.fori_loop(0, nchunk_c, ctx_body, 0)
    lax.fori_loop(0, half - 1, latent_body(False, True), 0)
    latent_body(False, False)(half - 1, 0)
    lax.fori_loop(half, nchunk, latent_body(True, False), 0)


def _ssd(xbcp, dt, dtt, z, xbcp_c, dt_c, dtt_c, conv_w, conv_b, dt_bias, a_log, d_skip, norm_w):
    bsz, seq, _ = z.shape
    lc = dt_c.shape[1]
    nh2 = 2 * SSD_HEADS
    full = lambda shape: pl.BlockSpec(shape, lambda b: (0,) * len(shape))
    per_b = lambda shape: pl.BlockSpec((1,) + shape, lambda b: (b, 0, 0))
    return pl.pallas_call(
        functools.partial(_ssd_kernel, seq=seq, lc=lc),
        grid=(bsz,),
        in_specs=[
            per_b((seq + CHUNK, XBC_WIDTH)), per_b((seq, nh2)), per_b((nh2, seq)), per_b((seq, SSD_WIDTH)),
            per_b((lc + CHUNK, XBC_WIDTH)), per_b((lc, nh2)), per_b((nh2, lc)),
            full((3, XBC_WIDTH)), full((1, XBC_WIDTH)),
            full((1, nh2)), full((nh2, 1)), full((1, nh2)), full((nh2, 1)),
            full((1, SSD_WIDTH)), full((1, SSD_WIDTH)),
        ],
        out_specs=per_b((seq, SSD_WIDTH)),
        out_shape=jax.ShapeDtypeStruct((bsz, seq, SSD_WIDTH), BF16),
        scratch_shapes=[
            pltpu.VMEM((seq, XBC_WIDTH), BF16),
            pltpu.VMEM((lc, XBC_WIDTH), BF16),
            pltpu.VMEM((seq, SSD_WIDTH), F32),
            pltpu.VMEM((2, SSD_GROUPS, SSD_WIDTH // SSD_GROUPS, SSD_STATE), F32),
        ],
        compiler_params=_params(("arbitrary",)),
        name="ssd",
    )(xbcp, dt, dtt, z, xbcp_c, dt_c, dtt_c, conv_w, conv_b.reshape(1, XBC_WIDTH),
      dt_bias.reshape(1, nh2), dt_bias.reshape(nh2, 1), a_log.reshape(1, nh2), a_log.reshape(nh2, 1),
      jnp.repeat(d_skip, SSD_HEAD_DIM).reshape(1, SSD_WIDTH), norm_w.reshape(1, SSD_WIDTH))


def _outproj_kernel(attn_ref, yssd_ref, x_ref, mod_ref, wa_ref, ws_ref, nw_ref, x1_ref, h2p_ref, *, tm, seq):
    m = pl.program_id(1)

    @pl.when(m == 0)
    def _():
        zeros = jnp.zeros((HALO, D_MODEL), BF16)
        h2p_ref[0, 0:HALO, :] = zeros
        h2p_ref[0, seq + HALO:seq + 2 * HALO, :] = zeros

    mod = mod_ref[0]
    mix = (jnp.dot(attn_ref[0], wa_ref[...], preferred_element_type=F32)
           + jnp.dot(yssd_ref[0], ws_ref[...], preferred_element_type=F32))
    x1 = x_ref[0] + mod[2:3] * mix
    x1_ref[0] = x1
    h2 = _rms(x1, nw_ref[...]) * (1.0 + mod[4:5]) + mod[3:4]
    h2p_ref[0, pl.ds(pl.multiple_of(HALO + m * tm, HALO), tm), :] = h2.astype(BF16)


def _outproj(attn, yssd, x, mod, w_attn, w_ssd, norm_w, *, tm):
    bsz, seq, d = x.shape
    return pl.pallas_call(
        functools.partial(_outproj_kernel, tm=tm, seq=seq),
        grid=(bsz, seq // tm),
        in_specs=[
            pl.BlockSpec((1, tm, NA_WIDTH), lambda b, m: (b, m, 0)),
            pl.BlockSpec((1, tm, SSD_WIDTH), lambda b, m: (b, m, 0)),
            pl.BlockSpec((1, tm, d), lambda b, m: (b, m, 0)),
            pl.BlockSpec((1, 6, d), lambda b, m: (b, 0, 0)),
            pl.BlockSpec(w_attn.shape, lambda b, m: (0, 0)),
            pl.BlockSpec(w_ssd.shape, lambda b, m: (0, 0)),
            pl.BlockSpec((1, d), lambda b, m: (0, 0)),
        ],
        out_specs=[pl.BlockSpec((1, tm, d), lambda b, m: (b, m, 0)),
                   pl.BlockSpec((1, seq + 2 * HALO, d), lambda b, m: (b, 0, 0))],
        out_shape=[jax.ShapeDtypeStruct((bsz, seq, d), F32),
                   jax.ShapeDtypeStruct((bsz, seq + 2 * HALO, d), BF16)],
        compiler_params=_params(("arbitrary", "arbitrary")),
        name="outproj",
    )(attn, yssd, x, mod, w_attn, w_ssd, norm_w.reshape(1, d))


def _ffn_kernel(h2p_ref, x1_ref, mod_ref, wup_ref, wd_ref, cw_ref, cb_ref, fnw_ref, o_ref, act_ref, *, tm, tf):
    m = pl.program_id(1)
    win = tm + 2 * HALO
    start = pl.multiple_of(m * tm, tm)
    hb = h2p_ref[0, pl.ds(start, win), :]
    body = slice(HALO, HALO + tm)
    for j in range(D_FF // tf):
        cols = slice(j * tf, (j + 1) * tf)
        gate = jnp.dot(hb, wup_ref[:, cols], preferred_element_type=F32)
        val = jnp.dot(hb[body], wup_ref[:, D_FF + j * tf:D_FF + (j + 1) * tf], preferred_element_type=F32)
        gc = (cw_ref[0:1, cols] * pltpu.roll(gate, 1, 0)[body] + cw_ref[1:2, cols] * gate[body]
              + cw_ref[2:3, cols] * pltpu.roll(gate, win - 1, 0)[body] + cb_ref[:, cols])
        act_ref[:, cols] = (_silu(gc) * val).astype(BF16)
    y = jnp.dot(act_ref[...], wd_ref[...], preferred_element_type=F32)
    o_ref[0] = _rms(x1_ref[0] + mod_ref[0][5:6] * y, fnw_ref[...])


def _ffn(h2p, x1, mod, w_up, w_down, conv_w, conv_b, final_w, *, tm, tf):
    bsz, seq, d = x1.shape
    dff = w_down.shape[0]
    once = pl.Buffered(1)
    return pl.pallas_call(
        functools.partial(_ffn_kernel, tm=tm, tf=tf),
        grid=(bsz, seq // tm),
        in_specs=[
            pl.BlockSpec((1, seq + 2 * HALO, d), lambda b, m: (b, 0, 0)),
            pl.BlockSpec((1, tm, d), lambda b, m: (b, m, 0)),
            pl.BlockSpec((1, 6, d), lambda b, m: (b, 0, 0)),
            pl.BlockSpec((d, 2 * dff), lambda b, m: (0, 0), pipeline_mode=once),
            pl.BlockSpec((dff, d), lambda b, m: (0, 0), pipeline_mode=once),
            pl.BlockSpec((3, dff), lambda b, m: (0, 0)),
            pl.BlockSpec((1, dff), lambda b, m: (0, 0)),
            pl.BlockSpec((1, d), lambda b, m: (0, 0)),
        ],
        out_specs=pl.BlockSpec((1, tm, d), lambda b, m: (b, m, 0)),
        out_shape=jax.ShapeDtypeStruct((bsz, seq, d), F32),
        scratch_shapes=[pltpu.VMEM((tm, dff), BF16)],
        compiler_params=_params(("arbitrary", "arbitrary")),
        name="ffn",
    )(h2p, x1, mod, w_up, w_down, conv_w, conv_b.reshape(1, dff), final_w.reshape(1, d))


def kernel(x, c, ctx, c_ctx, ada_w, ada_b, norm1_w, w_in, rpb, ssd_conv_w, ssd_conv_b, dt_bias, a_log,
           ssd_d, ssd_norm_w, w_out, norm2_w, ffn_w_up, ffn_conv_w, ffn_conv_b, ffn_w_down, final_norm_w):
    assert ada_w.shape[0] == 1, "single-layer kernel"
    bsz, seq, d = x.shape
    rows = seq // GRID_W
    nh2 = 2 * SSD_HEADS

    cc = jnp.concatenate([c, c_ctx[None]], axis=0)
    cc = jnp.pad(cc, ((0, (-cc.shape[0]) % 8), (0, 0)))
    mod_all = _ada(cc, ada_w[0], ada_b[0])
    mod = mod_all[:bsz].reshape(bsz, 6, d)
    mod_c = mod_all[bsz:bsz + 1].reshape(1, 6, d)

    w_in_b = w_in[0].astype(BF16)
    q, k, vt, z, xbcp, dt = _inproj(x, mod, norm1_w[0], w_in_b, _rope_tables(seq), latent=True, tm=512)
    kc, vct, xbcp_c, dt_c = _inproj(ctx, mod_c, norm1_w[0], w_in_b, None, latent=False, tm=ctx.shape[1])

    attn = _attention(q, k, vt, kc, vct, _bias_tables(rpb[0], rows))

    dtt = jnp.swapaxes(dt, 1, 2)
    dtt_c = jnp.swapaxes(dt_c, 1, 2)
    yssd = _ssd(xbcp, dt, dtt, z, xbcp_c, dt_c, dtt_c, ssd_conv_w[0], ssd_conv_b[0],
                dt_bias[0].reshape(nh2), a_log[0].reshape(nh2), ssd_d[0], ssd_norm_w[0])

    w_out_b = w_out[0].astype(BF16)
    x1, h2p = _outproj(attn, yssd, x, mod, w_out_b[:NA_WIDTH], w_out_b[NA_WIDTH:], norm2_w[0], tm=512)

    return _ffn(h2p, x1, mod, ffn_w_up[0].astype(BF16), ffn_w_down[0].astype(BF16),
                ffn_conv_w[0], ffn_conv_b[0], final_norm_w, tm=512, tf=256)
```
